```python
import math
import jax, jax.numpy as jnp
from jax import lax
import numpy as np

D_MODEL = 1024
BATCH = 32
SEQ = 2048
DEPTH = 1

MIX_WIDTH = D_MODEL
ATTN_WIDTH = MIX_WIDTH // 2
CONV_WIDTH = MIX_WIDTH - ATTN_WIDTH
DIFF_HEAD_DIM = 64
N_DIFF_HEADS = ATTN_WIDTH // (2 * DIFF_HEAD_DIM)
IN_WIDTH = 3 * ATTN_WIDTH + 2 * CONV_WIDTH
CONV_KERNEL = 31
D_FF = ((8 * D_MODEL // 3 + 127) // 128) * 128
FFN_CONV_KERNEL = 3
Q_BLOCK = 128
NORM_EPS = 1e-6
LN_EPS = 1e-5
NEG_INF = -1e30

kernel_name = "hymba_diffattn_conformer_convffn"


def lambda_init_fn(layer_idx):
    return 0.8 - 0.6 * math.exp(-0.3 * layer_idx)


def rms_norm(x, g, eps=NORM_EPS):
    xf = x.astype(jnp.float32)
    y = xf * lax.rsqrt(jnp.mean(xf * xf, axis=-1, keepdims=True) + eps)
    return (y * g.astype(jnp.float32)).astype(x.dtype)


def layer_norm(x, g, b, eps=LN_EPS):
    xf = x.astype(jnp.float32)
    mu = jnp.mean(xf, axis=-1, keepdims=True)
    var = jnp.mean(jnp.square(xf - mu), axis=-1, keepdims=True)
    y = (xf - mu) * lax.rsqrt(var + eps)
    return (y * g.astype(jnp.float32) + b.astype(jnp.float32)).astype(x.dtype)


def causal_depthwise_conv(x, w, b):
    k_width, channels = w.shape
    y = lax.conv_general_dilated(
        x, w[:, None, :].astype(x.dtype), window_strides=(1,),
        padding=[(k_width - 1, 0)],
        dimension_numbers=("NWC", "WIO", "NWC"),
        feature_group_count=channels)
    return y + b.astype(x.dtype)


def diff_attention(q, k, v, lq1, lk1, lq2, lk2, subln_g, lambda_init):
    B, S, _ = q.shape
    H, d = N_DIFF_HEADS, DIFF_HEAD_DIM
    out_dtype = q.dtype
    lam = (jnp.exp(jnp.sum(lq1.astype(jnp.float32) * lk1.astype(jnp.float32)))
           - jnp.exp(jnp.sum(lq2.astype(jnp.float32) * lk2.astype(jnp.float32)))
           + lambda_init)
    scale = d ** -0.5
    n_blocks = S // Q_BLOCK
    q_blocks = (q.reshape(B, n_blocks, Q_BLOCK, H, 2, d)
                 .transpose(1, 0, 2, 3, 4, 5).astype(jnp.float32) * scale)
    kf = k.reshape(B, S, H, 2, d).astype(jnp.float32)
    vf = v.reshape(B, S, H, 2 * d).astype(jnp.float32)
    k_pos = jnp.arange(S)

    def one_block(args):
        q_blk, blk = args
        s = jnp.einsum("bqhcd,bkhcd->bhcqk", q_blk, kf)
        q_pos = blk * Q_BLOCK + jnp.arange(Q_BLOCK)
        causal = k_pos[None, :] <= q_pos[:, None]
        p = jax.nn.softmax(jnp.where(causal, s, NEG_INF), axis=-1)
        a = p[:, :, 0] - lam * p[:, :, 1]
        return jnp.einsum("bhqk,bkhe->bqhe", a, vf)

    o = lax.map(one_block, (q_blocks, jnp.arange(n_blocks)))
    o = o.transpose(1, 0, 2, 3, 4).reshape(B, S, H, 2 * d)
    o = o * lax.rsqrt(jnp.mean(o * o, axis=-1, keepdims=True) + NORM_EPS)
    o = o * subln_g.astype(jnp.float32) * (1.0 - lambda_init)
    return o.reshape(B, S, H * 2 * d).astype(out_dtype)


def conformer_conv(val, gate, dw_w, dw_b, ln_g, ln_b):
    z = val * jax.nn.sigmoid(gate)
    z = causal_depthwise_conv(z, dw_w, dw_b)
    z = layer_norm(z, ln_g, ln_b)
    return jax.nn.silu(z)


def conv_ffn(u, w_up, conv_w, conv_b, w_down):
    up = u @ w_up
    up = causal_depthwise_conv(up, conv_w, conv_b)
    a, b = jnp.split(up, 2, axis=-1)
    return (jax.nn.silu(a) * b) @ w_down


def setup_inputs(seed: int = 0) -> dict:
    key = jax.random.key(seed)
    ks = jax.random.split(key, 20)
    f32 = jnp.float32

    def normal(k, shape, scale):
        return jax.random.normal(k, shape, f32) * scale

    def gain(k, shape):
        return 1.0 + 0.02 * jax.random.normal(k, shape, f32)

    return {
        "x": jax.random.normal(ks[0], (BATCH, SEQ, D_MODEL), f32),
        "attn_norm_g": gain(ks[1], (DEPTH, D_MODEL)),
        "w_in": normal(ks[2], (DEPTH, D_MODEL, IN_WIDTH), D_MODEL ** -0.5),
        "lambda_q1": normal(ks[3], (DEPTH, DIFF_HEAD_DIM), 0.1),
        "lambda_k1": normal(ks[4], (DEPTH, DIFF_HEAD_DIM), 0.1),
        "lambda_q2": normal(ks[5], (DEPTH, DIFF_HEAD_DIM), 0.1),
        "lambda_k2": normal(ks[6], (DEPTH, DIFF_HEAD_DIM), 0.1),
        "subln_g": gain(ks[7], (DEPTH, 2 * DIFF_HEAD_DIM)),
        "dw_conv_w": normal(ks[8], (DEPTH, CONV_KERNEL, CONV_WIDTH), CONV_KERNEL ** -0.5),
        "dw_conv_b": normal(ks[9], (DEPTH, CONV_WIDTH), 0.01),
        "conv_ln_g": gain(ks[10], (DEPTH, CONV_WIDTH)),
        "conv_ln_b": normal(ks[11], (DEPTH, CONV_WIDTH), 0.01),
        "w_out": normal(ks[12], (DEPTH, MIX_WIDTH, D_MODEL), MIX_WIDTH ** -0.5),
        "ffn_norm_g": gain(ks[13], (DEPTH, D_MODEL)),
        "w_up": normal(ks[14], (DEPTH, D_MODEL, 2 * D_FF), D_MODEL ** -0.5),
        "ffn_conv_w": normal(ks[15], (DEPTH, FFN_CONV_KERNEL, 2 * D_FF), FFN_CONV_KERNEL ** -0.5),
        "ffn_conv_b": normal(ks[16], (DEPTH, 2 * D_FF), 0.01),
        "w_down": normal(ks[17], (DEPTH, D_FF, D_MODEL), D_FF ** -0.5),
        "final_norm_g": gain(ks[18], (D_MODEL,)),
    }


def reference(x, attn_norm_g, w_in, lambda_q1, lambda_k1, lambda_q2, lambda_k2,
              subln_g, dw_conv_w, dw_conv_b, conv_ln_g, conv_ln_b, w_out,
              ffn_norm_g, w_up, ffn_conv_w, ffn_conv_b, w_down, final_norm_g):
    h = x
    split_points = [ATTN_WIDTH, 2 * ATTN_WIDTH, 3 * ATTN_WIDTH,
                    3 * ATTN_WIDTH + CONV_WIDTH]
    for l in range(DEPTH):
        lam_init = lambda_init_fn(l)
        u = rms_norm(h, attn_norm_g[l])
        proj = u @ w_in[l]
        q, k, v, c_val, c_gate = jnp.split(proj, split_points, axis=-1)
        y_attn = diff_attention(q, k, v, lambda_q1[l], lambda_k1[l],
                                lambda_q2[l], lambda_k2[l], subln_g[l], lam_init)
        y_conv = conformer_conv(c_val, c_gate, dw_conv_w[l], dw_conv_b[l],
                                conv_ln_g[l], conv_ln_b[l])
        mix = jnp.concatenate([y_attn, y_conv], axis=-1)
        h = h + mix @ w_out[l]
        u = rms_norm(h, ffn_norm_g[l])
        h = h + conv_ffn(u, w_up[l], ffn_conv_w[l], ffn_conv_b[l], w_down[l])
    return rms_norm(h, final_norm_g)
```

```python
import functools
import math

import jax
import jax.numpy as jnp
from jax import lax
from jax.experimental import pallas as pl
from jax.experimental.pallas import tpu as pltpu

F32 = jnp.float32
BF16 = jnp.bfloat16

HEAD_DIM = 64
HEAD_WIDTH = 2 * HEAD_DIM
NORM_EPS = 1e-6
LN_EPS = 1e-5
NEG_INF = -1e30
LOG2_E = math.log2(math.e)

LANES = 128
CONV_HALO = 32
CONV_ROWS = 32
FFN_HALO = 8
FFN_CHUNK = 256
MATMUL_COLS = 512
VMEM_LIMIT_BYTES = 56 * 1024 * 1024


def _rms_normalize(x, gain, eps):
    ms = jnp.mean(x * x, axis=-1, keepdims=True)
    return x * lax.rsqrt(ms + eps) * gain


def _column_blocks(width):
    return [slice(lo, lo + MATMUL_COLS) for lo in range(0, width, MATMUL_COLS)]


def _resident(shape):
    return pl.BlockSpec(shape, lambda *_: (0,) * len(shape), pipeline_mode=pl.Buffered(1))


_IN_HBM = pl.BlockSpec(memory_space=pl.ANY)


def _fetch_once(pairs, sems):
    @pl.when(pl.program_id(0) == 0)
    def _():
        copies = [pltpu.make_async_copy(src, dst, sems.at[i]) for i, (src, dst) in enumerate(pairs)]
        for cp in copies:
            cp.start()
        for cp in copies:
            cp.wait()


def _in_proj_kernel(x_ref, g_ref, w_hbm, q_ref, k_ref, v_ref, z_ref, w_ref, sems, *,
                    attn_width, conv_width):
    _fetch_once([(w_hbm, w_ref)], sems)
    u = _rms_normalize(x_ref[...], g_ref[...], NORM_EPS).astype(BF16)

    def proj(lo, width):
        return jnp.dot(u, w_ref[:, lo:lo + width], preferred_element_type=F32)

    a = attn_width
    q_ref[...] = (proj(0, a) * (HEAD_DIM ** -0.5 * LOG2_E)).astype(BF16)
    k_ref[...] = proj(a, a).astype(BF16)
    v_ref[...] = proj(2 * a, a).astype(BF16)
    val = proj(3 * a, conv_width)
    gate = proj(3 * a + conv_width, conv_width)
    z_ref[...] = (val * jax.nn.sigmoid(gate)).astype(BF16)


def _in_proj(x, norm_g, w_in, *, attn_width, conv_width, tm):
    bsz, seq, d_model = x.shape
    in_width = w_in.shape[1]
    assert seq % tm == 0
    kernel = functools.partial(_in_proj_kernel, attn_width=attn_width, conv_width=conv_width)
    tiles_per_seq = seq // tm
    tile = lambda w: pl.BlockSpec((None, tm, w),
                                  lambda t: (t // tiles_per_seq, t % tiles_per_seq, 0))
    out = lambda w: jax.ShapeDtypeStruct((bsz, seq, w), BF16)
    return pl.pallas_call(
        kernel,
        grid=(bsz * tiles_per_seq,),
        in_specs=[tile(d_model), _resident((1, d_model)), _IN_HBM],
        out_specs=[tile(attn_width), tile(attn_width), tile(attn_width), tile(conv_width)],
        out_shape=[out(attn_width), out(attn_width), out(attn_width), out(conv_width)],
        scratch_shapes=[pltpu.VMEM((d_model, in_width), BF16), pltpu.SemaphoreType.DMA((1,))],
        compiler_params=pltpu.CompilerParams(
            dimension_semantics=("arbitrary",),
            vmem_limit_bytes=VMEM_LIMIT_BYTES),
        name="in_proj",
    )(x, norm_g.reshape(1, d_model), w_in)


def _attn_kernel(lq1_ref, lk1_ref, lq2_ref, lk2_ref, q_ref, k_ref, v_ref, g_ref, o_ref,
                 vt_ref, *, tile, lambda_init):
    n_tiles = k_ref.shape[0] // tile

    for j in range(n_tiles):
        vt_ref[j] = v_ref[j * tile:(j + 1) * tile, :].astype(F32).T.astype(BF16)

    lam = (jnp.exp(jnp.sum(lq1_ref[...] * lk1_ref[...], axis=-1, keepdims=True))
           - jnp.exp(jnp.sum(lq2_ref[...] * lk2_ref[...], axis=-1, keepdims=True))
           + lambda_init)
    lane = lax.broadcasted_iota(jnp.int32, (tile, HEAD_WIDTH), 1)
    kpos = lax.broadcasted_iota(jnp.int32, (tile, tile), 0)
    qpos = lax.broadcasted_iota(jnp.int32, (tile, tile), 1)
    causal = kpos <= qpos

    for i in range(n_tiles):
        qf = q_ref[i * tile:(i + 1) * tile, :].astype(F32)
        q_comp = (jnp.where(lane < HEAD_DIM, qf, 0.0).astype(BF16),
                  jnp.where(lane >= HEAD_DIM, qf, 0.0).astype(BF16))
        m = [jnp.full((1, tile), NEG_INF, F32)] * 2
        l = [jnp.zeros((1, tile), F32)] * 2
        acc = [jnp.zeros((HEAD_WIDTH, tile), F32)] * 2
        for j in range(i + 1):
            kb = k_ref[j * tile:(j + 1) * tile, :]
            vt = vt_ref[j]
            for c in range(2):
                s = lax.dot_general(kb, q_comp[c], (((1,), (1,)), ((), ())),
                                    preferred_element_type=F32)
                if j == i:
                    s = jnp.where(causal, s, NEG_INF)
                m_new = jnp.maximum(m[c], jnp.max(s, axis=0, keepdims=True))
                alpha = jnp.exp2(m[c] - m_new)
                e = jnp.exp2(s - m_new)
                l[c] = alpha * l[c] + jnp.sum(e, axis=0, keepdims=True)
                acc[c] = alpha * acc[c] + jnp.dot(vt, e.astype(BF16), preferred_element_type=F32)
                m[c] = m_new
        o = acc[0] / l[0] - lam * (acc[1] / l[1])
        o = o * lax.rsqrt(jnp.mean(o * o, axis=0, keepdims=True) + NORM_EPS)
        o_ref[i * tile:(i + 1) * tile, :] = (o.T * g_ref[...] * (1.0 - lambda_init)).astype(BF16)


def _diff_attention(q, k, v, lq1, lk1, lq2, lk2, subln_g, lambda_init, *, tile):
    bsz, seq, width = q.shape
    n_heads = width // HEAD_WIDTH
    assert seq % tile == 0
    kernel = functools.partial(_attn_kernel, tile=tile, lambda_init=lambda_init)
    lam_spec = _resident((1, HEAD_DIM))
    head = pl.BlockSpec((None, seq, HEAD_WIDTH), lambda b, h: (b, 0, h))
    return pl.pallas_call(
        kernel,
        grid=(bsz, n_heads),
        in_specs=[lam_spec, lam_spec, lam_spec, lam_spec, head, head, head,
                  _resident((1, HEAD_WIDTH))],
        out_specs=head,
        out_shape=jax.ShapeDtypeStruct((bsz, seq, width), BF16),
        scratch_shapes=[pltpu.VMEM((seq // tile, HEAD_WIDTH, tile), BF16)],
        compiler_params=pltpu.CompilerParams(
            dimension_semantics=("arbitrary", "arbitrary"),
            vmem_limit_bytes=VMEM_LIMIT_BYTES),
        name="diff_attention",
    )(lq1.reshape(1, HEAD_DIM), lk1.reshape(1, HEAD_DIM), lq2.reshape(1, HEAD_DIM),
      lk2.reshape(1, HEAD_DIM), q, k, v, subln_g.reshape(1, HEAD_WIDTH))


def _depthwise_conv_units(halo, z, dww_ref, dwb_ref, zbuf_ref, conv_ref):
    tm, width = z.shape
    n_taps = dww_ref.shape[0]
    n_slabs = width // LANES
    for s in range(n_slabs):
        cols = slice(s * LANES, (s + 1) * LANES)
        zbuf_ref[s, 0:CONV_HALO, :] = halo[:, cols]
        zbuf_ref[s, CONV_HALO:CONV_HALO + tm, :] = z[:, cols]
    first_tap = CONV_HALO - (n_taps - 1)

    def unit(s, r0):
        cols = slice(s * LANES, (s + 1) * LANES)
        acc = jnp.broadcast_to(dwb_ref[:, cols], (CONV_ROWS, LANES))
        for j in range(n_taps):
            lo = r0 + first_tap + j
            acc = acc + dww_ref[j:j + 1, cols] * zbuf_ref[s, lo:lo + CONV_ROWS, :]
        conv_ref[r0:r0 + CONV_ROWS, cols] = acc

    return [functools.partial(unit, s, r0)
            for s in range(n_slabs) for r0 in range(0, tm, CONV_ROWS)]


def _norm_swish(y, gain, bias):
    mu = jnp.mean(y, axis=-1, keepdims=True)
    cen = y - mu
    var = jnp.mean(cen * cen, axis=-1, keepdims=True)
    y = cen * lax.rsqrt(var + LN_EPS) * gain + bias
    return (y * jax.nn.sigmoid(y)).astype(BF16)


def _out_ffn_kernel(x_ref, ya_ref, zc_ref, z_hbm, dww_ref, dwb_ref, lng_ref, lnb_ref,
                    wo_hbm, g_ref, wup_hbm, cw_ref, cb_ref, wdn_hbm, fg_ref, o_ref,
                    wo_ref, wup_ref, wdn_ref, w_sems,
                    zbuf_ref, conv_ref, ubuf_ref, carry_ref, gated_ref, znext_ref, znext_sem, *,
                    tiles_per_seq, n_tiles, n_chunks, final_norm):
    tm, d_model = x_ref.shape
    a = ya_ref.shape[1]
    width = zc_ref.shape[1]
    step = pl.program_id(0)
    fc = FFN_CHUNK
    n_slabs = 2 * fc // LANES
    conv_units = functools.partial(_depthwise_conv_units, dww_ref=dww_ref, dwb_ref=dwb_ref,
                                   zbuf_ref=zbuf_ref, conv_ref=conv_ref)

    def next_z_copy():
        nxt = jnp.minimum(step + 1, n_tiles - 1)
        row0 = pl.multiple_of((nxt % tiles_per_seq) * tm, tm)
        return pltpu.make_async_copy(z_hbm.at[nxt // tiles_per_seq, pl.ds(row0, tm), :],
                                     znext_ref, znext_sem)

    next_z_copy().start()
    _fetch_once([(wo_hbm, wo_ref), (wup_hbm, wup_ref), (wdn_hbm, wdn_ref)], w_sems)

    @pl.when(step == 0)
    def _():
        for unit in conv_units(jnp.zeros((CONV_HALO, width), F32), zc_ref[...].astype(F32)):
            unit()

    @pl.when(step % tiles_per_seq == 0)
    def _():
        carry_ref[...] = jnp.zeros(carry_ref.shape, F32)

    y_conv = _norm_swish(conv_ref[...], lng_ref[...], lnb_ref[...])

    ya = ya_ref[...]
    h = x_ref[...] + jnp.concatenate(
        [jnp.dot(ya, wo_ref[0:a, cols], preferred_element_type=F32)
         + jnp.dot(y_conv, wo_ref[a:, cols], preferred_element_type=F32)
         for cols in _column_blocks(d_model)], axis=-1)
    u = _rms_normalize(h, g_ref[...], NORM_EPS).astype(BF16)

    def up_proj(c):
        return jnp.dot(u, wup_ref[:, 2 * fc * c:2 * fc * (c + 1)], preferred_element_type=F32)

    def gate(c, up):
        buf = ubuf_ref.at[c % 2]
        parts = []
        for s in range(n_slabs):
            sl = slice(s * LANES, (s + 1) * LANES)
            buf[s, 0:FFN_HALO, :] = carry_ref[c, :, sl]
            buf[s, FFN_HALO:FFN_HALO + tm, :] = up[:, sl]
            carry_ref[c, :, sl] = up[tm - FFN_HALO:, sl]
            wcols = slice(2 * fc * c + s * LANES, 2 * fc * c + (s + 1) * LANES)
            conv = jnp.broadcast_to(cb_ref[:, wcols], (tm, LANES))
            for j in range(3):
                lo = FFN_HALO - 2 + j
                conv = conv + cw_ref[j:j + 1, wcols] * buf[s, lo:lo + tm, :]
            parts.append(conv)
        conv = jnp.concatenate(parts, axis=-1)
        ga = conv[:, :fc]
        gated_ref[:, fc * c:fc * (c + 1)] = (ga * jax.nn.sigmoid(ga) * conv[:, fc:]).astype(BF16)

    up = up_proj(0)
    for c in range(n_chunks):
        up_next = up_proj(c + 1) if c + 1 < n_chunks else None
        gate(c, up)
        up = up_next

    next_z_copy().wait()
    gated = gated_ref[...]
    acc = h + jnp.concatenate(
        [jnp.dot(gated, wdn_ref[:, cols], preferred_element_type=F32)
         for cols in _column_blocks(d_model)], axis=-1)

    next_starts_seq = (step + 1) % tiles_per_seq == 0
    halo = jnp.where(next_starts_seq, 0.0, zc_ref[tm - CONV_HALO:, :].astype(F32))
    for unit in conv_units(halo, znext_ref[...].astype(F32)):
        unit()

    if final_norm:
        acc = _rms_normalize(acc, fg_ref[...], NORM_EPS)
    o_ref[...] = acc


def _out_ffn(x, ya, z, dw_w, dw_b, ln_g, ln_b, w_out, norm_g, w_up, conv_w, conv_b, w_down,
             final_g, *, tm, final_norm):
    bsz, seq, d_model = x.shape
    a, cwid = ya.shape[2], z.shape[2]
    d_ff = w_down.shape[0]
    n_taps = dw_w.shape[0]
    assert seq % tm == 0 and d_ff % FFN_CHUNK == 0 and cwid % LANES == 0
    assert n_taps - 1 <= CONV_HALO <= tm and tm % CONV_ROWS == 0
    n_chunks = d_ff // FFN_CHUNK
    tiles_per_seq = seq // tm
    n_tiles = bsz * tiles_per_seq
    kernel = functools.partial(_out_ffn_kernel, tiles_per_seq=tiles_per_seq, n_tiles=n_tiles,
                               n_chunks=n_chunks, final_norm=final_norm)
    rows = lambda t: (t // tiles_per_seq, t % tiles_per_seq, 0)
    tile = lambda w: pl.BlockSpec((None, tm, w), rows)
    return pl.pallas_call(
        kernel,
        grid=(n_tiles,),
        in_specs=[
            tile(d_model), tile(a), tile(cwid), _IN_HBM,
            _resident((n_taps, cwid)), _resident((1, cwid)), _resident((1, cwid)),
            _resident((1, cwid)),
            _IN_HBM,
            _resident((1, d_model)),
            _IN_HBM,
            _resident((3, 2 * d_ff)),
            _resident((1, 2 * d_ff)),
            _IN_HBM,
            _resident((1, d_model)),
        ],
        out_specs=tile(d_model),
        out_shape=jax.ShapeDtypeStruct((bsz, seq, d_model), F32),
        scratch_shapes=[
            pltpu.VMEM((a + cwid, d_model), BF16),
            pltpu.VMEM((d_model, 2 * d_ff), BF16),
            pltpu.VMEM((d_ff, d_model), BF16),
            pltpu.SemaphoreType.DMA((3,)),
            pltpu.VMEM((cwid // LANES, CONV_HALO + tm, LANES), F32),
            pltpu.VMEM((tm, cwid), F32),
            pltpu.VMEM((2, 2 * FFN_CHUNK // LANES, FFN_HALO + tm, LANES), F32),
            pltpu.VMEM((n_chunks, FFN_HALO, 2 * FFN_CHUNK), F32),
            pltpu.VMEM((tm, d_ff), BF16),
            pltpu.VMEM((tm, cwid), BF16),
            pltpu.SemaphoreType.DMA(()),
        ],
        compiler_params=pltpu.CompilerParams(
            dimension_semantics=("arbitrary",),
            vmem_limit_bytes=VMEM_LIMIT_BYTES),
        name="out_ffn",
    )(x, ya, z, z, dw_w, dw_b.reshape(1, cwid), ln_g.reshape(1, cwid), ln_b.reshape(1, cwid),
      w_out, norm_g.reshape(1, d_model), w_up, conv_w, conv_b.reshape(1, 2 * d_ff),
      w_down, final_g.reshape(1, d_model))


def _group_ffn_columns(t, d_ff):
    lead = t.shape[:-1]
    n = d_ff // FFN_CHUNK
    t = t.reshape(lead + (2, n, FFN_CHUNK))
    return jnp.swapaxes(t, -3, -2).reshape(lead + (2 * d_ff,))


def _lambda_init(layer_idx):
    return 0.8 - 0.6 * math.exp(-0.3 * layer_idx)


def kernel(x, attn_norm_g, w_in, lambda_q1, lambda_k1, lambda_q2, lambda_k2, subln_g, dw_conv_w, dw_conv_b, conv_ln_g, conv_ln_b, w_out, ffn_norm_g, w_up, ffn_conv_w, ffn_conv_b, w_down, final_norm_g):
    depth = w_in.shape[0]
    d_model = x.shape[-1]
    conv_width = dw_conv_w.shape[-1]
    attn_width = d_model - conv_width
    d_ff = w_down.shape[1]
    h = x
    for l in range(depth):
        q, k, v, z = _in_proj(h, attn_norm_g[l], w_in[l].astype(BF16),
                              attn_width=attn_width, conv_width=conv_width, tm=512)
        y_attn = _diff_attention(q, k, v, lambda_q1[l], lambda_k1[l], lambda_q2[l], lambda_k2[l],
                                 subln_g[l], _lambda_init(l), tile=512)
        h = _out_ffn(
            h, y_attn, z, dw_conv_w[l], dw_conv_b[l], conv_ln_g[l], conv_ln_b[l],
            w_out[l].astype(BF16), ffn_norm_g[l],
            _group_ffn_columns(w_up[l], d_ff).astype(BF16),
            _group_ffn_columns(ffn_conv_w[l], d_ff), _group_ffn_columns(ffn_conv_b[l], d_ff),
            w_down[l].astype(BF16), final_norm_g, tm=512, final_norm=(l == depth - 1))
    return h
```

```python
import functools
import math

import jax
import jax.numpy as jnp
from jax import lax
from jax.experimental import pallas as pl
from jax.experimental.pallas import tpu as pltpu

F32 = jnp.float32
BF16 = jnp.bfloat16

HEAD_DIM = 64
HEAD_WIDTH = 2 * HEAD_DIM
NORM_EPS = 1e-6
LN_EPS = 1e-5
NEG_INF = -1e30
LOG2_E = math.log2(math.e)

SUM_ROWS = 16
LANES = 128
CONV_HALO = 32
CONV_ROWS = 32
FFN_HALO = 8
FFN_CHUNK = 256
MATMUL_COLS = 512
VMEM_LIMIT_BYTES = 56 * 1024 * 1024


def _rms_normalize(x, gain, eps):
    ms = jnp.mean(x * x, axis=-1, keepdims=True)
    return x * lax.rsqrt(ms + eps) * gain


def _column_blocks(width):
    return [slice(lo, lo + MATMUL_COLS) for lo in range(0, width, MATMUL_COLS)]


def _resident(shape):
    return pl.BlockSpec(shape, lambda *_: (0,) * len(shape), pipeline_mode=pl.Buffered(1))


_IN_HBM = pl.BlockSpec(memory_space=pl.ANY)


def _fetch_once(pairs, sems):
    @pl.when(pl.program_id(0) == 0)
    def _():
        copies = [pltpu.make_async_copy(src, dst, sems.at[i]) for i, (src, dst) in enumerate(pairs)]
        for cp in copies:
            cp.start()
        for cp in copies:
            cp.wait()


def _in_proj_kernel(x_ref, g_ref, w_hbm, q_ref, k_ref, v_ref, z_ref, w_ref, sems, *,
                    attn_width, conv_width):
    _fetch_once([(w_hbm, w_ref)], sems)
    u = _rms_normalize(x_ref[...], g_ref[...], NORM_EPS).astype(BF16)

    def proj(lo, width):
        return jnp.dot(u, w_ref[:, lo:lo + width], preferred_element_type=F32)

    a = attn_width
    q_ref[...] = (proj(0, a) * (HEAD_DIM ** -0.5 * LOG2_E)).astype(BF16)
    k_ref[...] = proj(a, a).astype(BF16)
    v_ref[...] = proj(2 * a, a).astype(BF16)
    val = proj(3 * a, conv_width)
    gate = proj(3 * a + conv_width, conv_width)
    z_ref[...] = (val * jax.nn.sigmoid(gate)).astype(BF16)


def _in_proj(x, norm_g, w_in, *, attn_width, conv_width, tm):
    bsz, seq, d_model = x.shape
    in_width = w_in.shape[1]
    assert seq % tm == 0
    kernel = functools.partial(_in_proj_kernel, attn_width=attn_width, conv_width=conv_width)
    tiles_per_seq = seq // tm
    tile = lambda w: pl.BlockSpec((None, tm, w),
                                  lambda t: (t // tiles_per_seq, t % tiles_per_seq, 0))
    out = lambda w: jax.ShapeDtypeStruct((bsz, seq, w), BF16)
    return pl.pallas_call(
        kernel,
        grid=(bsz * tiles_per_seq,),
        in_specs=[tile(d_model), _resident((1, d_model)), _IN_HBM],
        out_specs=[tile(attn_width), tile(attn_width), tile(attn_width), tile(conv_width)],
        out_shape=[out(attn_width), out(attn_width), out(attn_width), out(conv_width)],
        scratch_shapes=[pltpu.VMEM((d_model, in_width), BF16), pltpu.SemaphoreType.DMA((1,))],
        compiler_params=pltpu.CompilerParams(
            dimension_semantics=("arbitrary",),
            vmem_limit_bytes=VMEM_LIMIT_BYTES),
        name="in_proj",
    )(x, norm_g.reshape(1, d_model), w_in)


def _attn_kernel(lq1_ref, lk1_ref, lq2_ref, lk2_ref, q_ref, k_ref, v_ref, g_ref, o_ref,
                 vt_ref, *, tile, lambda_init):
    seq = k_ref.shape[0]
    n = seq // tile

    for j in range(n):
        vt = v_ref[j * tile:(j + 1) * tile, :].astype(F32).T.astype(BF16)
        vt_ref[j] = jnp.concatenate([vt, jnp.ones((SUM_ROWS, tile), BF16)], axis=0)

    lam = (jnp.exp(jnp.sum(lq1_ref[...] * lk1_ref[...], axis=-1, keepdims=True))
           - jnp.exp(jnp.sum(lq2_ref[...] * lk2_ref[...], axis=-1, keepdims=True))
           + lambda_init)
    lane = lax.broadcasted_iota(jnp.int32, (seq, HEAD_WIDTH), 1)
    kpos = lax.broadcasted_iota(jnp.int32, (tile, tile), 0)
    qpos = lax.broadcasted_iota(jnp.int32, (tile, tile), 1)
    causal = kpos <= qpos

    qf = q_ref[...].astype(F32)
    q_comp = (jnp.where(lane < HEAD_DIM, qf, 0.0).astype(BF16),
              jnp.where(lane >= HEAD_DIM, qf, 0.0).astype(BF16))
    m = [[jnp.full((1, tile), NEG_INF, F32) for _ in range(n)] for _ in range(2)]
    acc = [[jnp.zeros((HEAD_WIDTH + SUM_ROWS, tile), F32) for _ in range(n)] for _ in range(2)]

    for j in range(n):
        kb = k_ref[j * tile:(j + 1) * tile, :]
        nb = n - j
        q_cat = jnp.concatenate([q_comp[0][j * tile:, :], q_comp[1][j * tile:, :]], axis=0)
        s_all = lax.dot_general(kb, q_cat, (((1,), (1,)), ((), ())),
                                preferred_element_type=F32)
        probs, alphas = [], []
        for c in range(2):
            for b in range(j, n):
                col = (c * nb + b - j) * tile
                s = s_all[:, col:col + tile]
                if b == j:
                    s = jnp.where(causal, s, NEG_INF)
                m_new = jnp.maximum(m[c][b], jnp.max(s, axis=0, keepdims=True))
                alphas.append(jnp.exp2(m[c][b] - m_new))
                probs.append(jnp.exp2(s - m_new).astype(BF16))
                m[c][b] = m_new
        pv_all = jnp.dot(vt_ref[j], jnp.concatenate(probs, axis=1), preferred_element_type=F32)
        for c in range(2):
            for b in range(j, n):
                idx = c * nb + b - j
                acc[c][b] = alphas[idx] * acc[c][b] + pv_all[:, idx * tile:(idx + 1) * tile]

        num = [acc[c][j][:HEAD_WIDTH] for c in range(2)]
        den = [acc[c][j][HEAD_WIDTH:HEAD_WIDTH + 1] for c in range(2)]
        o = num[0] / den[0] - lam * (num[1] / den[1])
        o = o * lax.rsqrt(jnp.mean(o * o, axis=0, keepdims=True) + NORM_EPS)
        o_ref[j * tile:(j + 1) * tile, :] = (o.T * g_ref[...] * (1.0 - lambda_init)).astype(BF16)


def _diff_attention(q, k, v, lq1, lk1, lq2, lk2, subln_g, lambda_init, *, tile):
    bsz, seq, width = q.shape
    n_heads = width // HEAD_WIDTH
    assert seq % tile == 0
    kernel = functools.partial(_attn_kernel, tile=tile, lambda_init=lambda_init)
    lam_spec = _resident((1, HEAD_DIM))
    head = pl.BlockSpec((None, seq, HEAD_WIDTH), lambda b, h: (b, 0, h))
    return pl.pallas_call(
        kernel,
        grid=(bsz, n_heads),
        in_specs=[lam_spec, lam_spec, lam_spec, lam_spec, head, head, head,
                  _resident((1, HEAD_WIDTH))],
        out_specs=head,
        out_shape=jax.ShapeDtypeStruct((bsz, seq, width), BF16),
        scratch_shapes=[pltpu.VMEM((seq // tile, HEAD_WIDTH + SUM_ROWS, tile), BF16)],
        compiler_params=pltpu.CompilerParams(
            dimension_semantics=("arbitrary", "arbitrary"),
            vmem_limit_bytes=VMEM_LIMIT_BYTES),
        name="diff_attention",
    )(lq1.reshape(1, HEAD_DIM), lk1.reshape(1, HEAD_DIM), lq2.reshape(1, HEAD_DIM),
      lk2.reshape(1, HEAD_DIM), q, k, v, subln_g.reshape(1, HEAD_WIDTH))


def _depthwise_conv_units(halo, z, dww_ref, dwb_ref, zbuf_ref, conv_ref):
    tm, width = z.shape
    n_taps = dww_ref.shape[0]
    n_slabs = width // LANES
    for s in range(n_slabs):
        cols = slice(s * LANES, (s + 1) * LANES)
        zbuf_ref[s, 0:CONV_HALO, :] = halo[:, cols]
        zbuf_ref[s, CONV_HALO:CONV_HALO + tm, :] = z[:, cols]
    first_tap = CONV_HALO - (n_taps - 1)

    def unit(s, r0):
        cols = slice(s * LANES, (s + 1) * LANES)
        acc = jnp.broadcast_to(dwb_ref[:, cols], (CONV_ROWS, LANES))
        for j in range(n_taps):
            lo = r0 + first_tap + j
            acc = acc + dww_ref[j:j + 1, cols] * zbuf_ref[s, lo:lo + CONV_ROWS, :]
        conv_ref[r0:r0 + CONV_ROWS, cols] = acc

    return [functools.partial(unit, s, r0)
            for s in range(n_slabs) for r0 in range(0, tm, CONV_ROWS)]


def _norm_swish(y, gain, bias):
    mu = jnp.mean(y, axis=-1, keepdims=True)
    cen = y - mu
    var = jnp.mean(cen * cen, axis=-1, keepdims=True)
    y = cen * lax.rsqrt(var + LN_EPS) * gain + bias
    return (y * jax.nn.sigmoid(y)).astype(BF16)


def _out_ffn_kernel(x_ref, ya_ref, zc_ref, z_hbm, dww_ref, dwb_ref, lng_ref, lnb_ref,
                    wo_hbm, g_ref, wup_hbm, cw_ref, cb_ref, wdn_hbm, fg_ref, o_ref,
                    wo_ref, wup_ref, wdn_ref, w_sems,
                    zbuf_ref, conv_ref, ubuf_ref, carry_ref, gated_ref, znext_ref, znext_sem, *,
                    tiles_per_seq, n_tiles, n_chunks, final_norm):
    tm, d_model = x_ref.shape
    a = ya_ref.shape[1]
    width = zc_ref.shape[1]
    step = pl.program_id(0)
    fc = FFN_CHUNK
    n_slabs = 2 * fc // LANES
    conv_units = functools.partial(_depthwise_conv_units, dww_ref=dww_ref, dwb_ref=dwb_ref,
                                   zbuf_ref=zbuf_ref, conv_ref=conv_ref)

    def next_z_copy():
        nxt = jnp.minimum(step + 1, n_tiles - 1)
        row0 = pl.multiple_of((nxt % tiles_per_seq) * tm, tm)
        return pltpu.make_async_copy(z_hbm.at[nxt // tiles_per_seq, pl.ds(row0, tm), :],
                                     znext_ref, znext_sem)

    next_z_copy().start()
    d_ff = n_chunks * fc
    wup_copies = [(wup_hbm.at[:, pl.ds(part * d_ff + c * fc, fc)],
                   wup_ref.at[:, pl.ds((2 * c + part) * fc, fc)])
                  for c in range(n_chunks) for part in range(2)]
    _fetch_once([(wo_hbm, wo_ref), (wdn_hbm, wdn_ref)] + wup_copies, w_sems)

    @pl.when(step == 0)
    def _():
        for unit in conv_units(jnp.zeros((CONV_HALO, width), F32), zc_ref[...].astype(F32)):
            unit()

    @pl.when(step % tiles_per_seq == 0)
    def _():
        carry_ref[...] = jnp.zeros(carry_ref.shape, F32)

    y_conv = _norm_swish(conv_ref[...], lng_ref[...], lnb_ref[...])

    ya = ya_ref[...]
    h = x_ref[...] + jnp.concatenate(
        [jnp.dot(ya, wo_ref[0:a, cols], preferred_element_type=F32)
         + jnp.dot(y_conv, wo_ref[a:, cols], preferred_element_type=F32)
         for cols in _column_blocks(d_model)], axis=-1)
    u = _rms_normalize(h, g_ref[...], NORM_EPS).astype(BF16)

    def up_proj(c):
        return jnp.dot(u, wup_ref[:, 2 * fc * c:2 * fc * (c + 1)], preferred_element_type=F32)

    def gate(c, up):
        buf = ubuf_ref.at[c % 2]
        parts = []
        for s in range(n_slabs):
            sl = slice(s * LANES, (s + 1) * LANES)
            buf[s, 0:FFN_HALO, :] = carry_ref[c, :, sl]
            buf[s, FFN_HALO:FFN_HALO + tm, :] = up[:, sl]
            carry_ref[c, :, sl] = up[tm - FFN_HALO:, sl]
            wcols = slice(2 * fc * c + s * LANES, 2 * fc * c + (s + 1) * LANES)
            conv = jnp.broadcast_to(cb_ref[:, wcols], (tm, LANES))
            for j in range(3):
                lo = FFN_HALO - 2 + j
                conv = conv + cw_ref[j:j + 1, wcols] * buf[s, lo:lo + tm, :]
            parts.append(conv)
        conv = jnp.concatenate(parts, axis=-1)
        ga = conv[:, :fc]
        gated_ref[:, fc * c:fc * (c + 1)] = (ga * jax.nn.sigmoid(ga) * conv[:, fc:]).astype(BF16)

    up = up_proj(0)
    for c in range(n_chunks):
        up_next = up_proj(c + 1) if c + 1 < n_chunks else None
        gate(c, up)
        up = up_next

    next_z_copy().wait()
    gated = gated_ref[...]
    acc = h + jnp.concatenate(
        [jnp.dot(gated, wdn_ref[:, cols], preferred_element_type=F32)
         for cols in _column_blocks(d_model)], axis=-1)

    next_starts_seq = (step + 1) % tiles_per_seq == 0
    halo = jnp.where(next_starts_seq, 0.0, zc_ref[tm - CONV_HALO:, :].astype(F32))
    for unit in conv_units(halo, znext_ref[...].astype(F32)):
        unit()

    if final_norm:
        acc = _rms_normalize(acc, fg_ref[...], NORM_EPS)
    o_ref[...] = acc


def _out_ffn(x, ya, z, dw_w, dw_b, ln_g, ln_b, w_out, norm_g, w_up, conv_w, conv_b, w_down,
             final_g, *, tm, final_norm):
    bsz, seq, d_model = x.shape
    a, cwid = ya.shape[2], z.shape[2]
    d_ff = w_down.shape[0]
    n_taps = dw_w.shape[0]
    assert seq % tm == 0 and d_ff % FFN_CHUNK == 0 and cwid % LANES == 0
    assert n_taps - 1 <= CONV_HALO <= tm and tm % CONV_ROWS == 0
    n_chunks = d_ff // FFN_CHUNK
    tiles_per_seq = seq // tm
    n_tiles = bsz * tiles_per_seq
    kernel = functools.partial(_out_ffn_kernel, tiles_per_seq=tiles_per_seq, n_tiles=n_tiles,
                               n_chunks=n_chunks, final_norm=final_norm)
    rows = lambda t: (t // tiles_per_seq, t % tiles_per_seq, 0)
    tile = lambda w: pl.BlockSpec((None, tm, w), rows)
    return pl.pallas_call(
        kernel,
        grid=(n_tiles,),
        in_specs=[
            tile(d_model), tile(a), tile(cwid), _IN_HBM,
            _resident((n_taps, cwid)), _resident((1, cwid)), _resident((1, cwid)),
            _resident((1, cwid)),
            _IN_HBM,
            _resident((1, d_model)),
            _IN_HBM,
            _resident((3, 2 * d_ff)),
            _resident((1, 2 * d_ff)),
            _IN_HBM,
            _resident((1, d_model)),
        ],
        out_specs=tile(d_model),
        out_shape=jax.ShapeDtypeStruct((bsz, seq, d_model), F32),
        scratch_shapes=[
            pltpu.VMEM((a + cwid, d_model), BF16),
            pltpu.VMEM((d_model, 2 * d_ff), BF16),
            pltpu.VMEM((d_ff, d_model), BF16),
            pltpu.SemaphoreType.DMA((2 + 2 * n_chunks,)),
            pltpu.VMEM((cwid // LANES, CONV_HALO + tm, LANES), F32),
            pltpu.VMEM((tm, cwid), F32),
            pltpu.VMEM((2, 2 * FFN_CHUNK // LANES, FFN_HALO + tm, LANES), F32),
            pltpu.VMEM((n_chunks, FFN_HALO, 2 * FFN_CHUNK), F32),
            pltpu.VMEM((tm, d_ff), BF16),
            pltpu.VMEM((tm, cwid), BF16),
            pltpu.SemaphoreType.DMA(()),
        ],
        compiler_params=pltpu.CompilerParams(
            dimension_semantics=("arbitrary",),
            vmem_limit_bytes=VMEM_LIMIT_BYTES),
        name="out_ffn",
    )(x, ya, z, z, dw_w, dw_b.reshape(1, cwid), ln_g.reshape(1, cwid), ln_b.reshape(1, cwid),
      w_out, norm_g.reshape(1, d_model), w_up, conv_w, conv_b.reshape(1, 2 * d_ff),
      w_down, final_g.reshape(1, d_model))


def _group_ffn_columns(t, d_ff):
    lead = t.shape[:-1]
    n = d_ff // FFN_CHUNK
    t = t.reshape(lead + (2, n, FFN_CHUNK))
    return jnp.swapaxes(t, -3, -2).reshape(lead + (2 * d_ff,))


def _lambda_init(layer_idx):
    return 0.8 - 0.6 * math.exp(-0.3 * layer_idx)


def kernel(x, attn_norm_g, w_in, lambda_q1, lambda_k1, lambda_q2, lambda_k2, subln_g, dw_conv_w, dw_conv_b, conv_ln_g, conv_ln_b, w_out, ffn_norm_g, w_up, ffn_conv_w, ffn_conv_b, w_down, final_norm_g):
    depth = w_in.shape[0]
    d_model = x.shape[-1]
    conv_width = dw_conv_w.shape[-1]
    attn_width = d_model - conv_width
    d_ff = w_down.shape[1]
    h = x
    for l in range(depth):
        q, k, v, z = _in_proj(h, attn_norm_g[l], w_in[l].astype(BF16),
                              attn_width=attn_width, conv_width=conv_width, tm=512)
        y_attn = _diff_attention(q, k, v, lambda_q1[l], lambda_k1[l], lambda_q2[l], lambda_k2[l],
                                 subln_g[l], _lambda_init(l), tile=256)
        h = _out_ffn(
            h, y_attn, z, dw_conv_w[l], dw_conv_b[l], conv_ln_g[l], conv_ln_b[l],
            w_out[l].astype(BF16), ffn_norm_g[l],
            w_up[l].astype(BF16),
            _group_ffn_columns(ffn_conv_w[l], d_ff), _group_ffn_columns(ffn_conv_b[l], d_ff),
            w_down[l].astype(BF16), final_norm_g, tm=512, final_norm=(l == depth - 1))
    return h
```

```python
import functools
import math

import jax
import jax.numpy as jnp
from jax import lax
from jax.experimental import pallas as pl
from jax.experimental.pallas import tpu as pltpu

F32 = jnp.float32
BF16 = jnp.bfloat16

HEAD_DIM = 64
HEAD_WIDTH = 2 * HEAD_DIM
NORM_EPS = 1e-6
LN_EPS = 1e-5
NEG_INF = -1e30
LOG2_E = math.log2(math.e)

SUM_ROWS = 16
LANES = 128
CONV_HALO = 32
CONV_ROWS = 32
FFN_HALO = 8
FFN_CHUNK = 256
MATMUL_COLS = 512
MATMUL_ROWS = 256
VMEM_LIMIT_BYTES = 56 * 1024 * 1024


def _rms_normalize(x, gain, eps):
    ms = jnp.mean(x * x, axis=-1, keepdims=True)
    return x * lax.rsqrt(ms + eps) * gain


def _column_blocks(width):
    return [slice(lo, lo + MATMUL_COLS) for lo in range(0, width, MATMUL_COLS)]


def _resident(shape):
    return pl.BlockSpec(shape, lambda *_: (0,) * len(shape), pipeline_mode=pl.Buffered(1))


_IN_HBM = pl.BlockSpec(memory_space=pl.ANY)


def _fetch_once(pairs, sems):
    @pl.when(pl.program_id(0) == 0)
    def _():
        copies = [pltpu.make_async_copy(src, dst, sems.at[i]) for i, (src, dst) in enumerate(pairs)]
        for cp in copies:
            cp.start()
        for cp in copies:
            cp.wait()


def _in_proj_kernel(x_ref, g_ref, w_hbm, q_ref, k_ref, v_ref, z_ref, w_ref, sems, u_ref, *,
                    attn_width, conv_width):
    _fetch_once([(w_hbm, w_ref)], sems)
    tm = x_ref.shape[0]
    a = attn_width
    for r0 in range(0, tm, MATMUL_ROWS):
        rows = slice(r0, r0 + MATMUL_ROWS)
        u_ref[rows, :] = _rms_normalize(x_ref[rows, :], g_ref[...], NORM_EPS).astype(BF16)

        def proj(lo, width):
            return jnp.dot(u_ref[rows, :], w_ref[:, lo:lo + width], preferred_element_type=F32)

        q_ref[rows, :] = (proj(0, a) * (HEAD_DIM ** -0.5 * LOG2_E)).astype(BF16)
        k_ref[rows, :] = proj(a, a).astype(BF16)
        v_ref[rows, :] = proj(2 * a, a).astype(BF16)
        val = proj(3 * a, conv_width)
        gate = proj(3 * a + conv_width, conv_width)
        z_ref[rows, :] = (val * jax.nn.sigmoid(gate)).astype(BF16)


def _in_proj(x, norm_g, w_in, *, attn_width, conv_width, tm):
    bsz, seq, d_model = x.shape
    in_width = w_in.shape[1]
    assert seq % tm == 0 and tm % MATMUL_ROWS == 0
    kernel = functools.partial(_in_proj_kernel, attn_width=attn_width, conv_width=conv_width)
    tiles_per_seq = seq // tm
    tile = lambda w: pl.BlockSpec((None, tm, w),
                                  lambda t: (t // tiles_per_seq, t % tiles_per_seq, 0))
    out = lambda w: jax.ShapeDtypeStruct((bsz, seq, w), BF16)
    return pl.pallas_call(
        kernel,
        grid=(bsz * tiles_per_seq,),
        in_specs=[tile(d_model), _resident((1, d_model)), _IN_HBM],
        out_specs=[tile(attn_width), tile(attn_width), tile(attn_width), tile(conv_width)],
        out_shape=[out(attn_width), out(attn_width), out(attn_width), out(conv_width)],
        scratch_shapes=[pltpu.VMEM((d_model, in_width), BF16), pltpu.SemaphoreType.DMA((1,)),
                        pltpu.VMEM((tm, d_model), BF16)],
        compiler_params=pltpu.CompilerParams(
            dimension_semantics=("arbitrary",),
            vmem_limit_bytes=VMEM_LIMIT_BYTES),
        name="in_proj",
    )(x, norm_g.reshape(1, d_model), w_in)


def _attn_kernel(lq1_ref, lk1_ref, lq2_ref, lk2_ref, q_ref, k_ref, v_ref, g_ref, o_ref,
                 vt_ref, *, tile, lambda_init):
    seq = k_ref.shape[0]
    n = seq // tile

    for j in range(n):
        vt = v_ref[j * tile:(j + 1) * tile, :].astype(F32).T.astype(BF16)
        vt_ref[j] = jnp.concatenate([vt, jnp.ones((SUM_ROWS, tile), BF16)], axis=0)

    lam = (jnp.exp(jnp.sum(lq1_ref[...] * lk1_ref[...], axis=-1, keepdims=True))
           - jnp.exp(jnp.sum(lq2_ref[...] * lk2_ref[...], axis=-1, keepdims=True))
           + lambda_init)
    lane = lax.broadcasted_iota(jnp.int32, (seq, HEAD_WIDTH), 1)
    kpos = lax.broadcasted_iota(jnp.int32, (tile, tile), 0)
    qpos = lax.broadcasted_iota(jnp.int32, (tile, tile), 1)
    causal = kpos <= qpos

    qf = q_ref[...].astype(F32)
    q_comp = (jnp.where(lane < HEAD_DIM, qf, 0.0).astype(BF16),
              jnp.where(lane >= HEAD_DIM, qf, 0.0).astype(BF16))
    m = [[jnp.full((1, tile), NEG_INF, F32) for _ in range(n)] for _ in range(2)]
    acc = [[jnp.zeros((HEAD_WIDTH + SUM_ROWS, tile), F32) for _ in range(n)] for _ in range(2)]

    for j in range(n):
        kb = k_ref[j * tile:(j + 1) * tile, :]
        nb = n - j
        q_cat = jnp.concatenate([q_comp[0][j * tile:, :], q_comp[1][j * tile:, :]], axis=0)
        s_all = lax.dot_general(kb, q_cat, (((1,), (1,)), ((), ())),
                                preferred_element_type=F32)
        probs, alphas = [], []
        for c in range(2):
            for b in range(j, n):
                col = (c * nb + b - j) * tile
                s = s_all[:, col:col + tile]
                if b == j:
                    s = jnp.where(causal, s, NEG_INF)
                m_new = jnp.maximum(m[c][b], jnp.max(s, axis=0, keepdims=True))
                alphas.append(jnp.exp2(m[c][b] - m_new))
                probs.append(jnp.exp2(s - m_new).astype(BF16))
                m[c][b] = m_new
        pv_all = jnp.dot(vt_ref[j], jnp.concatenate(probs, axis=1), preferred_element_type=F32)
        for c in range(2):
            for b in range(j, n):
                idx = c * nb + b - j
                acc[c][b] = alphas[idx] * acc[c][b] + pv_all[:, idx * tile:(idx + 1) * tile]

        num = [acc[c][j][:HEAD_WIDTH] for c in range(2)]
        den = [acc[c][j][HEAD_WIDTH:HEAD_WIDTH + 1] for c in range(2)]
        o = num[0] / den[0] - lam * (num[1] / den[1])
        o = o * lax.rsqrt(jnp.mean(o * o, axis=0, keepdims=True) + NORM_EPS)
        o_ref[j * tile:(j + 1) * tile, :] = (o.T * g_ref[...] * (1.0 - lambda_init)).astype(BF16)


def _diff_attention(q, k, v, lq1, lk1, lq2, lk2, subln_g, lambda_init, *, tile):
    bsz, seq, width = q.shape
    n_heads = width // HEAD_WIDTH
    assert seq % tile == 0
    kernel = functools.partial(_attn_kernel, tile=tile, lambda_init=lambda_init)
    lam_spec = _resident((1, HEAD_DIM))
    head = pl.BlockSpec((None, seq, HEAD_WIDTH), lambda b, h: (b, 0, h))
    return pl.pallas_call(
        kernel,
        grid=(bsz, n_heads),
        in_specs=[lam_spec, lam_spec, lam_spec, lam_spec, head, head, head,
                  _resident((1, HEAD_WIDTH))],
        out_specs=head,
        out_shape=jax.ShapeDtypeStruct((bsz, seq, width), BF16),
        scratch_shapes=[pltpu.VMEM((seq // tile, HEAD_WIDTH + SUM_ROWS, tile), BF16)],
        compiler_params=pltpu.CompilerParams(
            dimension_semantics=("arbitrary", "arbitrary"),
            vmem_limit_bytes=VMEM_LIMIT_BYTES),
        name="diff_attention",
    )(lq1.reshape(1, HEAD_DIM), lk1.reshape(1, HEAD_DIM), lq2.reshape(1, HEAD_DIM),
      lk2.reshape(1, HEAD_DIM), q, k, v, subln_g.reshape(1, HEAD_WIDTH))


def _depthwise_conv_units(halo, z, dww_ref, dwb_ref, zbuf_ref, conv_ref):
    tm, width = z.shape
    n_taps = dww_ref.shape[0]
    n_slabs = width // LANES
    for s in range(n_slabs):
        cols = slice(s * LANES, (s + 1) * LANES)
        zbuf_ref[s, 0:CONV_HALO, :] = halo[:, cols]
        zbuf_ref[s, CONV_HALO:CONV_HALO + tm, :] = z[:, cols]
    first_tap = CONV_HALO - (n_taps - 1)

    def unit(s, r0):
        cols = slice(s * LANES, (s + 1) * LANES)
        acc = jnp.broadcast_to(dwb_ref[:, cols], (CONV_ROWS, LANES))
        for j in range(n_taps):
            lo = r0 + first_tap + j
            acc = acc + dww_ref[j:j + 1, cols] * zbuf_ref[s, lo:lo + CONV_ROWS, :]
        conv_ref[r0:r0 + CONV_ROWS, cols] = acc

    return [functools.partial(unit, s, r0)
            for s in range(n_slabs) for r0 in range(0, tm, CONV_ROWS)]


def _norm_swish(y, gain, bias):
    mu = jnp.mean(y, axis=-1, keepdims=True)
    cen = y - mu
    var = jnp.mean(cen * cen, axis=-1, keepdims=True)
    y = cen * lax.rsqrt(var + LN_EPS) * gain + bias
    return (y * jax.nn.sigmoid(y)).astype(BF16)


def _out_ffn_kernel(x_ref, ya_ref, zc_ref, z_hbm, dww_ref, dwb_ref, lng_ref, lnb_ref,
                    wo_hbm, g_ref, wup_hbm, cw_ref, cb_ref, wdn_hbm, fg_ref, o_ref,
                    wo_ref, wup_ref, wdn_ref, w_sems, u_ref,
                    zbuf_ref, conv_ref, ubuf_ref, carry_ref, gated_ref, znext_ref, znext_sem, *,
                    tiles_per_seq, n_tiles, n_chunks, final_norm):
    tm, d_model = x_ref.shape
    a = ya_ref.shape[1]
    width = zc_ref.shape[1]
    step = pl.program_id(0)
    fc = FFN_CHUNK
    n_slabs = 2 * fc // LANES
    conv_units = functools.partial(_depthwise_conv_units, dww_ref=dww_ref, dwb_ref=dwb_ref,
                                   zbuf_ref=zbuf_ref, conv_ref=conv_ref)

    def next_z_copy():
        nxt = jnp.minimum(step + 1, n_tiles - 1)
        row0 = pl.multiple_of((nxt % tiles_per_seq) * tm, tm)
        return pltpu.make_async_copy(z_hbm.at[nxt // tiles_per_seq, pl.ds(row0, tm), :],
                                     znext_ref, znext_sem)

    next_z_copy().start()
    d_ff = n_chunks * fc
    wup_copies = [(wup_hbm.at[:, pl.ds(part * d_ff + c * fc, fc)],
                   wup_ref.at[:, pl.ds((2 * c + part) * fc, fc)])
                  for c in range(n_chunks) for part in range(2)]
    _fetch_once([(wo_hbm, wo_ref), (wdn_hbm, wdn_ref)] + wup_copies, w_sems)

    @pl.when(step == 0)
    def _():
        for unit in conv_units(jnp.zeros((CONV_HALO, width), F32), zc_ref[...].astype(F32)):
            unit()

    @pl.when(step % tiles_per_seq == 0)
    def _():
        carry_ref[...] = jnp.zeros(carry_ref.shape, F32)

    y_conv = _norm_swish(conv_ref[...], lng_ref[...], lnb_ref[...])

    row_blocks = [slice(r0, r0 + MATMUL_ROWS) for r0 in range(0, tm, MATMUL_ROWS)]
    h = x_ref[...] + jnp.concatenate(
        [jnp.concatenate(
            [jnp.dot(ya_ref[rows, :], wo_ref[0:a, cols], preferred_element_type=F32)
             + jnp.dot(y_conv[rows, :], wo_ref[a:, cols], preferred_element_type=F32)
             for cols in _column_blocks(d_model)], axis=-1)
         for rows in row_blocks], axis=0)
    u_ref[...] = _rms_normalize(h, g_ref[...], NORM_EPS).astype(BF16)

    def up_proj(c):
        return jnp.concatenate(
            [jnp.dot(u_ref[rows, :], wup_ref[:, 2 * fc * c:2 * fc * (c + 1)],
                     preferred_element_type=F32) for rows in row_blocks], axis=0)

    def gate(c, up):
        buf = ubuf_ref.at[c % 2]
        parts = []
        for s in range(n_slabs):
            sl = slice(s * LANES, (s + 1) * LANES)
            buf[s, 0:FFN_HALO, :] = carry_ref[c, :, sl]
            buf[s, FFN_HALO:FFN_HALO + tm, :] = up[:, sl]
            carry_ref[c, :, sl] = up[tm - FFN_HALO:, sl]
            wcols = slice(2 * fc * c + s * LANES, 2 * fc * c + (s + 1) * LANES)
            conv = jnp.broadcast_to(cb_ref[:, wcols], (tm, LANES))
            for j in range(3):
                lo = FFN_HALO - 2 + j
                conv = conv + cw_ref[j:j + 1, wcols] * buf[s, lo:lo + tm, :]
            parts.append(conv)
        conv = jnp.concatenate(parts, axis=-1)
        ga = conv[:, :fc]
        gated_ref[:, fc * c:fc * (c + 1)] = (ga * jax.nn.sigmoid(ga) * conv[:, fc:]).astype(BF16)

    up = up_proj(0)
    for c in range(n_chunks):
        up_next = up_proj(c + 1) if c + 1 < n_chunks else None
        gate(c, up)
        up = up_next

    next_z_copy().wait()
    for cols in _column_blocks(d_model):
        for rows in row_blocks:
            o_ref[rows, cols] = jnp.dot(gated_ref[rows, :], wdn_ref[:, cols],
                                        preferred_element_type=F32)

    next_starts_seq = (step + 1) % tiles_per_seq == 0
    halo = jnp.where(next_starts_seq, 0.0, zc_ref[tm - CONV_HALO:, :].astype(F32))
    for unit in conv_units(halo, znext_ref[...].astype(F32)):
        unit()

    acc = h + o_ref[...]
    if final_norm:
        acc = _rms_normalize(acc, fg_ref[...], NORM_EPS)
    o_ref[...] = acc


def _out_ffn(x, ya, z, dw_w, dw_b, ln_g, ln_b, w_out, norm_g, w_up, conv_w, conv_b, w_down,
             final_g, *, tm, final_norm):
    bsz, seq, d_model = x.shape
    a, cwid = ya.shape[2], z.shape[2]
    d_ff = w_down.shape[0]
    n_taps = dw_w.shape[0]
    assert seq % tm == 0 and d_ff % FFN_CHUNK == 0 and cwid % LANES == 0
    assert n_taps - 1 <= CONV_HALO <= tm and tm % CONV_ROWS == 0 and tm % MATMUL_ROWS == 0
    n_chunks = d_ff // FFN_CHUNK
    tiles_per_seq = seq // tm
    n_tiles = bsz * tiles_per_seq
    kernel = functools.partial(_out_ffn_kernel, tiles_per_seq=tiles_per_seq, n_tiles=n_tiles,
                               n_chunks=n_chunks, final_norm=final_norm)
    rows = lambda t: (t // tiles_per_seq, t % tiles_per_seq, 0)
    tile = lambda w: pl.BlockSpec((None, tm, w), rows)
    return pl.pallas_call(
        kernel,
        grid=(n_tiles,),
        in_specs=[
            tile(d_model), tile(a), tile(cwid), _IN_HBM,
            _resident((n_taps, cwid)), _resident((1, cwid)), _resident((1, cwid)),
            _resident((1, cwid)),
            _IN_HBM,
            _resident((1, d_model)),
            _IN_HBM,
            _resident((3, 2 * d_ff)),
            _resident((1, 2 * d_ff)),
            _IN_HBM,
            _resident((1, d_model)),
        ],
        out_specs=tile(d_model),
        out_shape=jax.ShapeDtypeStruct((bsz, seq, d_model), F32),
        scratch_shapes=[
            pltpu.VMEM((a + cwid, d_model), BF16),
            pltpu.VMEM((d_model, 2 * d_ff), BF16),
            pltpu.VMEM((d_ff, d_model), BF16),
            pltpu.SemaphoreType.DMA((2 + 2 * n_chunks,)),
            pltpu.VMEM((tm, d_model), BF16),
            pltpu.VMEM((cwid // LANES, CONV_HALO + tm, LANES), F32),
            pltpu.VMEM((tm, cwid), F32),
            pltpu.VMEM((2, 2 * FFN_CHUNK // LANES, FFN_HALO + tm, LANES), F32),
            pltpu.VMEM((n_chunks, FFN_HALO, 2 * FFN_CHUNK), F32),
            pltpu.VMEM((tm, d_ff), BF16),
            pltpu.VMEM((tm, cwid), BF16),
            pltpu.SemaphoreType.DMA(()),
        ],
        compiler_params=pltpu.CompilerParams(
            dimension_semantics=("arbitrary",),
            vmem_limit_bytes=VMEM_LIMIT_BYTES),
        name="out_ffn",
    )(x, ya, z, z, dw_w, dw_b.reshape(1, cwid), ln_g.reshape(1, cwid), ln_b.reshape(1, cwid),
      w_out, norm_g.reshape(1, d_model), w_up, conv_w, conv_b.reshape(1, 2 * d_ff),
      w_down, final_g.reshape(1, d_model))


def _group_ffn_columns(t, d_ff):
    lead = t.shape[:-1]
    n = d_ff // FFN_CHUNK
    t = t.reshape(lead + (2, n, FFN_CHUNK))
    return jnp.swapaxes(t, -3, -2).reshape(lead + (2 * d_ff,))


def _lambda_init(layer_idx):
    return 0.8 - 0.6 * math.exp(-0.3 * layer_idx)


def kernel(x, attn_norm_g, w_in, lambda_q1, lambda_k1, lambda_q2, lambda_k2, subln_g, dw_conv_w, dw_conv_b, conv_ln_g, conv_ln_b, w_out, ffn_norm_g, w_up, ffn_conv_w, ffn_conv_b, w_down, final_norm_g):
    depth = w_in.shape[0]
    d_model = x.shape[-1]
    conv_width = dw_conv_w.shape[-1]
    attn_width = d_model - conv_width
    d_ff = w_down.shape[1]
    h = x
    for l in range(depth):
        q, k, v, z = _in_proj(h, attn_norm_g[l], w_in[l].astype(BF16),
                              attn_width=attn_width, conv_width=conv_width, tm=1024)
        y_attn = _diff_attention(q, k, v, lambda_q1[l], lambda_k1[l], lambda_q2[l], lambda_k2[l],
                                 subln_g[l], _lambda_init(l), tile=256)
        h = _out_ffn(
            h, y_attn, z, dw_conv_w[l], dw_conv_b[l], conv_ln_g[l], conv_ln_b[l],
            w_out[l].astype(BF16), ffn_norm_g[l],
            w_up[l].astype(BF16),
            _group_ffn_columns(ffn_conv_w[l], d_ff), _group_ffn_columns(ffn_conv_b[l], d_ff),
            w_down[l].astype(BF16), final_norm_g, tm=512, final_norm=(l == depth - 1))
    return h
```

```python
import functools
import math

import jax
import jax.numpy as jnp
from jax import lax
from jax.experimental import pallas as pl
from jax.experimental.pallas import tpu as pltpu

F32 = jnp.float32
BF16 = jnp.bfloat16

HEAD_DIM = 64
HEAD_WIDTH = 2 * HEAD_DIM
NORM_EPS = 1e-6
LN_EPS = 1e-5
NEG_INF = -1e30
LOG2_E = math.log2(math.e)

SUM_ROWS = 16
LANES = 128
CONV_HALO = 32
CONV_ROWS = 16
FFN_HALO = 8
FFN_CHUNK = 256
MATMUL_COLS = 512
MATMUL_ROWS = 256
VMEM_LIMIT_BYTES = 56 * 1024 * 1024


def _rms_normalize(x, gain, eps):
    ms = jnp.mean(x * x, axis=-1, keepdims=True)
    return x * lax.rsqrt(ms + eps) * gain


def _column_blocks(width):
    return [slice(lo, lo + MATMUL_COLS) for lo in range(0, width, MATMUL_COLS)]


def _resident(shape):
    return pl.BlockSpec(shape, lambda *_: (0,) * len(shape), pipeline_mode=pl.Buffered(1))


_IN_HBM = pl.BlockSpec(memory_space=pl.ANY)


def _fetch_once(pairs, sems):
    @pl.when(pl.program_id(0) == 0)
    def _():
        copies = [pltpu.make_async_copy(src, dst, sems.at[i]) for i, (src, dst) in enumerate(pairs)]
        for cp in copies:
            cp.start()
        for cp in copies:
            cp.wait()


def _in_proj_kernel(x_ref, g_ref, w_hbm, q_ref, k_ref, v_ref, z_ref, w_ref, sems, u_ref, *,
                    attn_width, conv_width):
    _fetch_once([(w_hbm, w_ref)], sems)
    tm = x_ref.shape[0]
    a = attn_width
    for r0 in range(0, tm, MATMUL_ROWS):
        rows = slice(r0, r0 + MATMUL_ROWS)
        u_ref[rows, :] = _rms_normalize(x_ref[rows, :], g_ref[...], NORM_EPS).astype(BF16)

        def proj(lo, width):
            return jnp.dot(u_ref[rows, :], w_ref[:, lo:lo + width], preferred_element_type=F32)

        q_ref[rows, :] = (proj(0, a) * (HEAD_DIM ** -0.5 * LOG2_E)).astype(BF16)
        k_ref[rows, :] = proj(a, a).astype(BF16)
        v_ref[rows, :] = proj(2 * a, a).astype(BF16)
        val = proj(3 * a, conv_width)
        gate = proj(3 * a + conv_width, conv_width)
        z_ref[rows, :] = (val * jax.nn.sigmoid(gate)).astype(BF16)


def _in_proj(x, norm_g, w_in, *, attn_width, conv_width, tm):
    bsz, seq, d_model = x.shape
    in_width = w_in.shape[1]
    assert seq % tm == 0 and tm % MATMUL_ROWS == 0
    kernel = functools.partial(_in_proj_kernel, attn_width=attn_width, conv_width=conv_width)
    tiles_per_seq = seq // tm
    tile = lambda w: pl.BlockSpec((None, tm, w),
                                  lambda t: (t // tiles_per_seq, t % tiles_per_seq, 0))
    out = lambda w: jax.ShapeDtypeStruct((bsz, seq, w), BF16)
    return pl.pallas_call(
        kernel,
        grid=(bsz * tiles_per_seq,),
        in_specs=[tile(d_model), _resident((1, d_model)), _IN_HBM],
        out_specs=[tile(attn_width), tile(attn_width), tile(attn_width), tile(conv_width)],
        out_shape=[out(attn_width), out(attn_width), out(attn_width), out(conv_width)],
        scratch_shapes=[pltpu.VMEM((d_model, in_width), BF16), pltpu.SemaphoreType.DMA((1,)),
                        pltpu.VMEM((tm, d_model), BF16)],
        compiler_params=pltpu.CompilerParams(
            dimension_semantics=("arbitrary",),
            vmem_limit_bytes=VMEM_LIMIT_BYTES),
        name="in_proj",
    )(x, norm_g.reshape(1, d_model), w_in)


def _attn_kernel(lq1_ref, lk1_ref, lq2_ref, lk2_ref, q_ref, k_ref, v_ref, g_ref, o_ref,
                 vt_ref, *, tile, lambda_init):
    seq = k_ref.shape[0]
    n = seq // tile

    for j in range(n):
        vt = v_ref[j * tile:(j + 1) * tile, :].astype(F32).T.astype(BF16)
        vt_ref[j] = jnp.concatenate([vt, jnp.ones((SUM_ROWS, tile), BF16)], axis=0)

    lam = (jnp.exp(jnp.sum(lq1_ref[...] * lk1_ref[...], axis=-1, keepdims=True))
           - jnp.exp(jnp.sum(lq2_ref[...] * lk2_ref[...], axis=-1, keepdims=True))
           + lambda_init)
    lane = lax.broadcasted_iota(jnp.int32, (seq, HEAD_WIDTH), 1)
    kpos = lax.broadcasted_iota(jnp.int32, (tile, tile), 0)
    qpos = lax.broadcasted_iota(jnp.int32, (tile, tile), 1)
    causal = kpos <= qpos

    qf = q_ref[...].astype(F32)
    q_comp = (jnp.where(lane < HEAD_DIM, qf, 0.0).astype(BF16),
              jnp.where(lane >= HEAD_DIM, qf, 0.0).astype(BF16))
    m = [[jnp.full((1, tile), NEG_INF, F32) for _ in range(n)] for _ in range(2)]
    acc = [[jnp.zeros((HEAD_WIDTH + SUM_ROWS, tile), F32) for _ in range(n)] for _ in range(2)]

    for j in range(n):
        kb = k_ref[j * tile:(j + 1) * tile, :]
        nb = n - j
        q_cat = jnp.concatenate([q_comp[0][j * tile:, :], q_comp[1][j * tile:, :]], axis=0)
        s_all = lax.dot_general(kb, q_cat, (((1,), (1,)), ((), ())),
                                preferred_element_type=F32)
        probs, alphas = [], []
        for c in range(2):
            for b in range(j, n):
                col = (c * nb + b - j) * tile
                s = s_all[:, col:col + tile]
                if b == j:
                    s = jnp.where(causal, s, NEG_INF)
                m_new = jnp.maximum(m[c][b], jnp.max(s, axis=0, keepdims=True))
                alphas.append(jnp.exp2(m[c][b] - m_new))
                probs.append(jnp.exp2(s - m_new).astype(BF16))
                m[c][b] = m_new
        pv_all = jnp.dot(vt_ref[j], jnp.concatenate(probs, axis=1), preferred_element_type=F32)
        for c in range(2):
            for b in range(j, n):
                idx = c * nb + b - j
                acc[c][b] = alphas[idx] * acc[c][b] + pv_all[:, idx * tile:(idx + 1) * tile]

        num = [acc[c][j][:HEAD_WIDTH] for c in range(2)]
        den = [acc[c][j][HEAD_WIDTH:HEAD_WIDTH + 1] for c in range(2)]
        o = num[0] / den[0] - lam * (num[1] / den[1])
        o = o * lax.rsqrt(jnp.mean(o * o, axis=0, keepdims=True) + NORM_EPS)
        o_ref[j * tile:(j + 1) * tile, :] = (o.T * g_ref[...] * (1.0 - lambda_init)).astype(BF16)


def _diff_attention(q, k, v, lq1, lk1, lq2, lk2, subln_g, lambda_init, *, tile):
    bsz, seq, width = q.shape
    n_heads = width // HEAD_WIDTH
    assert seq % tile == 0
    kernel = functools.partial(_attn_kernel, tile=tile, lambda_init=lambda_init)
    lam_spec = _resident((1, HEAD_DIM))
    head = pl.BlockSpec((None, seq, HEAD_WIDTH), lambda b, h: (b, 0, h))
    return pl.pallas_call(
        kernel,
        grid=(bsz, n_heads),
        in_specs=[lam_spec, lam_spec, lam_spec, lam_spec, head, head, head,
                  _resident((1, HEAD_WIDTH))],
        out_specs=head,
        out_shape=jax.ShapeDtypeStruct((bsz, seq, width), BF16),
        scratch_shapes=[pltpu.VMEM((seq // tile, HEAD_WIDTH + SUM_ROWS, tile), BF16)],
        compiler_params=pltpu.CompilerParams(
            dimension_semantics=("arbitrary", "arbitrary"),
            vmem_limit_bytes=VMEM_LIMIT_BYTES),
        name="diff_attention",
    )(lq1.reshape(1, HEAD_DIM), lk1.reshape(1, HEAD_DIM), lq2.reshape(1, HEAD_DIM),
      lk2.reshape(1, HEAD_DIM), q, k, v, subln_g.reshape(1, HEAD_WIDTH))


def _depthwise_conv_units(halo, z, dww_ref, dwb_ref, zbuf_ref, conv_ref):
    tm, width = z.shape
    n_taps = dww_ref.shape[0]
    n_slabs = width // LANES
    for s in range(n_slabs):
        cols = slice(s * LANES, (s + 1) * LANES)
        zbuf_ref[s, 0:CONV_HALO, :] = halo[:, cols]
        zbuf_ref[s, CONV_HALO:CONV_HALO + tm, :] = z[:, cols]
    first_tap = CONV_HALO - (n_taps - 1)

    def unit(s, r0, after=None):
        cols = slice(s * LANES, (s + 1) * LANES)
        acc = jnp.broadcast_to(dwb_ref[:, cols], (CONV_ROWS, LANES))
        if after is not None:
            zero = (lax.bitcast_convert_type(after, jnp.uint32) >> 16) >> 16
            acc = lax.bitcast_convert_type(lax.bitcast_convert_type(acc, jnp.uint32) + zero, F32)
        for j in range(n_taps):
            lo = r0 + first_tap + j
            acc = acc + dww_ref[j:j + 1, cols] * zbuf_ref[s, lo:lo + CONV_ROWS, :]
        conv_ref[r0:r0 + CONV_ROWS, cols] = acc
        return acc

    return [functools.partial(unit, s, r0)
            for s in range(n_slabs) for r0 in range(0, tm, CONV_ROWS)]


def _norm_swish(y, gain, bias):
    mu = jnp.mean(y, axis=-1, keepdims=True)
    cen = y - mu
    var = jnp.mean(cen * cen, axis=-1, keepdims=True)
    y = cen * lax.rsqrt(var + LN_EPS) * gain + bias
    return (y * jax.nn.sigmoid(y)).astype(BF16)


def _out_ffn_kernel(x_ref, ya_ref, zc_ref, z_hbm, dww_ref, dwb_ref, lng_ref, lnb_ref,
                    wo_hbm, g_ref, wup_hbm, cw_ref, cb_ref, wdn_hbm, fg_ref, o_ref,
                    wo_ref, wup_ref, wdn_ref, w_sems, u_ref,
                    zbuf_ref, conv_ref, ubuf_ref, carry_ref, gated_ref, znext_ref, znext_sem, *,
                    tiles_per_seq, n_tiles, n_chunks, final_norm):
    tm, d_model = x_ref.shape
    a = ya_ref.shape[1]
    width = zc_ref.shape[1]
    step = pl.program_id(0)
    fc = FFN_CHUNK
    n_slabs = 2 * fc // LANES
    conv_units = functools.partial(_depthwise_conv_units, dww_ref=dww_ref, dwb_ref=dwb_ref,
                                   zbuf_ref=zbuf_ref, conv_ref=conv_ref)

    def next_z_copy():
        nxt = jnp.minimum(step + 1, n_tiles - 1)
        row0 = pl.multiple_of((nxt % tiles_per_seq) * tm, tm)
        return pltpu.make_async_copy(z_hbm.at[nxt // tiles_per_seq, pl.ds(row0, tm), :],
                                     znext_ref, znext_sem)

    next_z_copy().start()
    d_ff = n_chunks * fc
    wup_copies = [(wup_hbm.at[:, pl.ds(part * d_ff + c * fc, fc)],
                   wup_ref.at[:, pl.ds((2 * c + part) * fc, fc)])
                  for c in range(n_chunks) for part in range(2)]
    _fetch_once([(wo_hbm, wo_ref), (wdn_hbm, wdn_ref)] + wup_copies, w_sems)

    @pl.when(step == 0)
    def _():
        for unit in conv_units(jnp.zeros((CONV_HALO, width), F32), zc_ref[...].astype(F32)):
            unit()

    @pl.when(step % tiles_per_seq == 0)
    def _():
        carry_ref[...] = jnp.zeros(carry_ref.shape, F32)

    y_conv = _norm_swish(conv_ref[...], lng_ref[...], lnb_ref[...])

    row_blocks = [slice(r0, r0 + MATMUL_ROWS) for r0 in range(0, tm, MATMUL_ROWS)]
    h = x_ref[...] + jnp.concatenate(
        [jnp.concatenate(
            [jnp.dot(ya_ref[rows, :], wo_ref[0:a, cols], preferred_element_type=F32)
             + jnp.dot(y_conv[rows, :], wo_ref[a:, cols], preferred_element_type=F32)
             for cols in _column_blocks(d_model)], axis=-1)
         for rows in row_blocks], axis=0)
    u_ref[...] = _rms_normalize(h, g_ref[...], NORM_EPS).astype(BF16)

    def up_proj(c):
        return jnp.concatenate(
            [jnp.dot(u_ref[rows, :], wup_ref[:, 2 * fc * c:2 * fc * (c + 1)],
                     preferred_element_type=F32) for rows in row_blocks], axis=0)

    def gate(c, up):
        buf = ubuf_ref.at[c % 2]
        parts = []
        for s in range(n_slabs):
            sl = slice(s * LANES, (s + 1) * LANES)
            buf[s, 0:FFN_HALO, :] = carry_ref[c, :, sl]
            buf[s, FFN_HALO:FFN_HALO + tm, :] = up[:, sl]
            carry_ref[c, :, sl] = up[tm - FFN_HALO:, sl]
            wcols = slice(2 * fc * c + s * LANES, 2 * fc * c + (s + 1) * LANES)
            conv = jnp.broadcast_to(cb_ref[:, wcols], (tm, LANES))
            for j in range(3):
                lo = FFN_HALO - 2 + j
                conv = conv + cw_ref[j:j + 1, wcols] * buf[s, lo:lo + tm, :]
            parts.append(conv)
        conv = jnp.concatenate(parts, axis=-1)
        ga = conv[:, :fc]
        gated_ref[:, fc * c:fc * (c + 1)] = (ga * jax.nn.sigmoid(ga) * conv[:, fc:]).astype(BF16)

    up = up_proj(0)
    for c in range(n_chunks):
        up_next = up_proj(c + 1) if c + 1 < n_chunks else None
        gate(c, up)
        up = up_next

    next_z_copy().wait()
    for cols in _column_blocks(d_model):
        for rows in row_blocks:
            o_ref[rows, cols] = jnp.dot(gated_ref[rows, :], wdn_ref[:, cols],
                                        preferred_element_type=F32)

    next_starts_seq = (step + 1) % tiles_per_seq == 0
    halo = jnp.where(next_starts_seq, 0.0, zc_ref[tm - CONV_HALO:, :].astype(F32))
    done = None
    for unit in conv_units(halo, znext_ref[...].astype(F32)):
        done = unit(after=done)

    acc = h + o_ref[...]
    if final_norm:
        acc = _rms_normalize(acc, fg_ref[...], NORM_EPS)
    o_ref[...] = acc


def _out_ffn(x, ya, z, dw_w, dw_b, ln_g, ln_b, w_out, norm_g, w_up, conv_w, conv_b, w_down,
             final_g, *, tm, final_norm):
    bsz, seq, d_model = x.shape
    a, cwid = ya.shape[2], z.shape[2]
    d_ff = w_down.shape[0]
    n_taps = dw_w.shape[0]
    assert seq % tm == 0 and d_ff % FFN_CHUNK == 0 and cwid % LANES == 0
    assert n_taps - 1 <= CONV_HALO <= tm and tm % CONV_ROWS == 0 and tm % MATMUL_ROWS == 0
    n_chunks = d_ff // FFN_CHUNK
    tiles_per_seq = seq // tm
    n_tiles = bsz * tiles_per_seq
    kernel = functools.partial(_out_ffn_kernel, tiles_per_seq=tiles_per_seq, n_tiles=n_tiles,
                               n_chunks=n_chunks, final_norm=final_norm)
    rows = lambda t: (t // tiles_per_seq, t % tiles_per_seq, 0)
    tile = lambda w: pl.BlockSpec((None, tm, w), rows)
    return pl.pallas_call(
        kernel,
        grid=(n_tiles,),
        in_specs=[
            tile(d_model), tile(a), tile(cwid), _IN_HBM,
            _resident((n_taps, cwid)), _resident((1, cwid)), _resident((1, cwid)),
            _resident((1, cwid)),
            _IN_HBM,
            _resident((1, d_model)),
            _IN_HBM,
            _resident((3, 2 * d_ff)),
            _resident((1, 2 * d_ff)),
            _IN_HBM,
            _resident((1, d_model)),
        ],
        out_specs=tile(d_model),
        out_shape=jax.ShapeDtypeStruct((bsz, seq, d_model), F32),
        scratch_shapes=[
            pltpu.VMEM((a + cwid, d_model), BF16),
            pltpu.VMEM((d_model, 2 * d_ff), BF16),
            pltpu.VMEM((d_ff, d_model), BF16),
            pltpu.SemaphoreType.DMA((2 + 2 * n_chunks,)),
            pltpu.VMEM((tm, d_model), BF16),
            pltpu.VMEM((cwid // LANES, CONV_HALO + tm, LANES), F32),
            pltpu.VMEM((tm, cwid), F32),
            pltpu.VMEM((2, 2 * FFN_CHUNK // LANES, FFN_HALO + tm, LANES), F32),
            pltpu.VMEM((n_chunks, FFN_HALO, 2 * FFN_CHUNK), F32),
            pltpu.VMEM((tm, d_ff), BF16),
            pltpu.VMEM((tm, cwid), BF16),
            pltpu.SemaphoreType.DMA(()),
        ],
        compiler_params=pltpu.CompilerParams(
            dimension_semantics=("arbitrary",),
            vmem_limit_bytes=VMEM_LIMIT_BYTES),
        name="out_ffn",
    )(x, ya, z, z, dw_w, dw_b.reshape(1, cwid), ln_g.reshape(1, cwid), ln_b.reshape(1, cwid),
      w_out, norm_g.reshape(1, d_model), w_up, conv_w, conv_b.reshape(1, 2 * d_ff),
      w_down, final_g.reshape(1, d_model))


def _group_ffn_columns(t, d_ff):
    lead = t.shape[:-1]
    n = d_ff // FFN_CHUNK
    t = t.reshape(lead + (2, n, FFN_CHUNK))
    return jnp.swapaxes(t, -3, -2).reshape(lead + (2 * d_ff,))


def _lambda_init(layer_idx):
    return 0.8 - 0.6 * math.exp(-0.3 * layer_idx)


def kernel(x, attn_norm_g, w_in, lambda_q1, lambda_k1, lambda_q2, lambda_k2, subln_g, dw_conv_w, dw_conv_b, conv_ln_g, conv_ln_b, w_out, ffn_norm_g, w_up, ffn_conv_w, ffn_conv_b, w_down, final_norm_g):
    depth = w_in.shape[0]
    d_model = x.shape[-1]
    conv_width = dw_conv_w.shape[-1]
    attn_width = d_model - conv_width
    d_ff = w_down.shape[1]
    h = x
    for l in range(depth):
        q, k, v, z = _in_proj(h, attn_norm_g[l], w_in[l].astype(BF16),
                              attn_width=attn_width, conv_width=conv_width, tm=1024)
        y_attn = _diff_attention(q, k, v, lambda_q1[l], lambda_k1[l], lambda_q2[l], lambda_k2[l],
                                 subln_g[l], _lambda_init(l), tile=256)
        h = _out_ffn(
            h, y_attn, z, dw_conv_w[l], dw_conv_b[l], conv_ln_g[l], conv_ln_b[l],
            w_out[l].astype(BF16), ffn_norm_g[l],
            w_up[l].astype(BF16),
            _group_ffn_columns(ffn_conv_w[l], d_ff), _group_ffn_columns(ffn_conv_b[l], d_ff),
            w_down[l].astype(BF16), final_norm_g, tm=512, final_norm=(l == depth - 1))
    return h
```

```python
import functools
import math

import jax
import jax.numpy as jnp
from jax import lax
from jax.experimental import pallas as pl
from jax.experimental.pallas import tpu as pltpu

F32 = jnp.float32
BF16 = jnp.bfloat16

HEAD_DIM = 64
HEAD_WIDTH = 2 * HEAD_DIM
NORM_EPS = 1e-6
LN_EPS = 1e-5
NEG_INF = -1e30
LOG2_E = math.log2(math.e)

SUM_ROWS = 16
LANES = 128
CONV_HALO = 32
SUBLANES = 8
CONV_ROWS = 24
FFN_HALO = 8
FFN_CHUNK = 256
MATMUL_COLS = 512
MATMUL_ROWS = 256
VMEM_LIMIT_BYTES = 56 * 1024 * 1024


def _rms_normalize(x, gain, eps):
    ms = jnp.mean(x * x, axis=-1, keepdims=True)
    return x * lax.rsqrt(ms + eps) * gain


def _column_blocks(width):
    return [slice(lo, lo + MATMUL_COLS) for lo in range(0, width, MATMUL_COLS)]


def _resident(shape):
    return pl.BlockSpec(shape, lambda *_: (0,) * len(shape), pipeline_mode=pl.Buffered(1))


_IN_HBM = pl.BlockSpec(memory_space=pl.ANY)


def _fetch_once(pairs, sems):
    @pl.when(pl.program_id(0) == 0)
    def _():
        copies = [pltpu.make_async_copy(src, dst, sems.at[i]) for i, (src, dst) in enumerate(pairs)]
        for cp in copies:
            cp.start()
        for cp in copies:
            cp.wait()


def _in_proj_kernel(x_ref, g_ref, w_hbm, q_ref, k_ref, v_ref, z_ref, w_ref, sems, u_ref, *,
                    attn_width, conv_width):
    _fetch_once([(w_hbm, w_ref)], sems)
    tm = x_ref.shape[0]
    a = attn_width
    for r0 in range(0, tm, MATMUL_ROWS):
        rows = slice(r0, r0 + MATMUL_ROWS)
        u_ref[rows, :] = _rms_normalize(x_ref[rows, :], g_ref[...], NORM_EPS).astype(BF16)

        def proj(lo, width):
            return jnp.dot(u_ref[rows, :], w_ref[:, lo:lo + width], preferred_element_type=F32)

        q_ref[rows, :] = (proj(0, a) * (HEAD_DIM ** -0.5 * LOG2_E)).astype(BF16)
        k_ref[rows, :] = proj(a, a).astype(BF16)
        v_ref[rows, :] = proj(2 * a, a).astype(BF16)
        val = proj(3 * a, conv_width)
        gate = proj(3 * a + conv_width, conv_width)
        z_ref[rows, :] = (val * jax.nn.sigmoid(gate)).astype(BF16)


def _in_proj(x, norm_g, w_in, *, attn_width, conv_width, tm):
    bsz, seq, d_model = x.shape
    in_width = w_in.shape[1]
    assert seq % tm == 0 and tm % MATMUL_ROWS == 0
    kernel = functools.partial(_in_proj_kernel, attn_width=attn_width, conv_width=conv_width)
    tiles_per_seq = seq // tm
    tile = lambda w: pl.BlockSpec((None, tm, w),
                                  lambda t: (t // tiles_per_seq, t % tiles_per_seq, 0))
    out = lambda w: jax.ShapeDtypeStruct((bsz, seq, w), BF16)
    return pl.pallas_call(
        kernel,
        grid=(bsz * tiles_per_seq,),
        in_specs=[tile(d_model), _resident((1, d_model)), _IN_HBM],
        out_specs=[tile(attn_width), tile(attn_width), tile(attn_width), tile(conv_width)],
        out_shape=[out(attn_width), out(attn_width), out(attn_width), out(conv_width)],
        scratch_shapes=[pltpu.VMEM((d_model, in_width), BF16), pltpu.SemaphoreType.DMA((1,)),
                        pltpu.VMEM((tm, d_model), BF16)],
        compiler_params=pltpu.CompilerParams(
            dimension_semantics=("arbitrary",),
            vmem_limit_bytes=VMEM_LIMIT_BYTES),
        name="in_proj",
    )(x, norm_g.reshape(1, d_model), w_in)


def _attn_kernel(lq1_ref, lk1_ref, lq2_ref, lk2_ref, q_ref, k_ref, v_ref, g_ref, o_ref,
                 vt_ref, *, tile, lambda_init):
    seq = k_ref.shape[0]
    n = seq // tile

    for j in range(n):
        vt = v_ref[j * tile:(j + 1) * tile, :].astype(F32).T.astype(BF16)
        vt_ref[j] = jnp.concatenate([vt, jnp.ones((SUM_ROWS, tile), BF16)], axis=0)

    lam = (jnp.exp(jnp.sum(lq1_ref[...] * lk1_ref[...], axis=-1, keepdims=True))
           - jnp.exp(jnp.sum(lq2_ref[...] * lk2_ref[...], axis=-1, keepdims=True))
           + lambda_init)
    lane = lax.broadcasted_iota(jnp.int32, (seq, HEAD_WIDTH), 1)
    kpos = lax.broadcasted_iota(jnp.int32, (tile, tile), 0)
    qpos = lax.broadcasted_iota(jnp.int32, (tile, tile), 1)
    causal = kpos <= qpos

    qf = q_ref[...].astype(F32)
    q_comp = (jnp.where(lane < HEAD_DIM, qf, 0.0).astype(BF16),
              jnp.where(lane >= HEAD_DIM, qf, 0.0).astype(BF16))
    m = [[jnp.full((1, tile), NEG_INF, F32) for _ in range(n)] for _ in range(2)]
    acc = [[jnp.zeros((HEAD_WIDTH + SUM_ROWS, tile), F32) for _ in range(n)] for _ in range(2)]

    for j in range(n):
        kb = k_ref[j * tile:(j + 1) * tile, :]
        nb = n - j
        q_cat = jnp.concatenate([q_comp[0][j * tile:, :], q_comp[1][j * tile:, :]], axis=0)
        s_all = lax.dot_general(kb, q_cat, (((1,), (1,)), ((), ())),
                                preferred_element_type=F32)
        probs, alphas = [], []
        for c in range(2):
            for b in range(j, n):
                col = (c * nb + b - j) * tile
                s = s_all[:, col:col + tile]
                if b == j:
                    s = jnp.where(causal, s, NEG_INF)
                m_new = jnp.maximum(m[c][b], jnp.max(s, axis=0, keepdims=True))
                alphas.append(jnp.exp2(m[c][b] - m_new))
                probs.append(jnp.exp2(s - m_new).astype(BF16))
                m[c][b] = m_new
        pv_all = jnp.dot(vt_ref[j], jnp.concatenate(probs, axis=1), preferred_element_type=F32)
        for c in range(2):
            for b in range(j, n):
                idx = c * nb + b - j
                acc[c][b] = alphas[idx] * acc[c][b] + pv_all[:, idx * tile:(idx + 1) * tile]

        num = [acc[c][j][:HEAD_WIDTH] for c in range(2)]
        den = [acc[c][j][HEAD_WIDTH:HEAD_WIDTH + 1] for c in range(2)]
        o = num[0] / den[0] - lam * (num[1] / den[1])
        o = o * lax.rsqrt(jnp.mean(o * o, axis=0, keepdims=True) + NORM_EPS)
        o_ref[j * tile:(j + 1) * tile, :] = (o.T * g_ref[...] * (1.0 - lambda_init)).astype(BF16)


def _diff_attention(q, k, v, lq1, lk1, lq2, lk2, subln_g, lambda_init, *, tile):
    bsz, seq, width = q.shape
    n_heads = width // HEAD_WIDTH
    assert seq % tile == 0
    kernel = functools.partial(_attn_kernel, tile=tile, lambda_init=lambda_init)
    lam_spec = _resident((1, HEAD_DIM))
    head = pl.BlockSpec((None, seq, HEAD_WIDTH), lambda b, h: (b, 0, h))
    return pl.pallas_call(
        kernel,
        grid=(bsz, n_heads),
        in_specs=[lam_spec, lam_spec, lam_spec, lam_spec, head, head, head,
                  _resident((1, HEAD_WIDTH))],
        out_specs=head,
        out_shape=jax.ShapeDtypeStruct((bsz, seq, width), BF16),
        scratch_shapes=[pltpu.VMEM((seq // tile, HEAD_WIDTH + SUM_ROWS, tile), BF16)],
        compiler_params=pltpu.CompilerParams(
            dimension_semantics=("arbitrary", "arbitrary"),
            vmem_limit_bytes=VMEM_LIMIT_BYTES),
        name="diff_attention",
    )(lq1.reshape(1, HEAD_DIM), lk1.reshape(1, HEAD_DIM), lq2.reshape(1, HEAD_DIM),
      lk2.reshape(1, HEAD_DIM), q, k, v, subln_g.reshape(1, HEAD_WIDTH))


def _depthwise_conv_units(halo, z, dww_ref, dwb_ref, zbuf_ref, conv_ref):
    tm, width = z.shape
    n_taps = dww_ref.shape[0]
    n_slabs = width // LANES
    for s in range(n_slabs):
        cols = slice(s * LANES, (s + 1) * LANES)
        zbuf_ref[s, 0:CONV_HALO, :] = halo[:, cols]
        zbuf_ref[s, CONV_HALO:CONV_HALO + tm, :] = z[:, cols]
    first_tap = CONV_HALO - (n_taps - 1)

    def unit(s, r0, n_rows, after=None):
        cols = slice(s * LANES, (s + 1) * LANES)
        acc = jnp.broadcast_to(dwb_ref[:, cols], (n_rows, LANES))
        if after is not None:
            zero = (lax.bitcast_convert_type(after[0:SUBLANES], jnp.uint32) >> 16) >> 16
            zero = jnp.concatenate([zero] * (n_rows // SUBLANES), axis=0)
            acc = lax.bitcast_convert_type(lax.bitcast_convert_type(acc, jnp.uint32) + zero, F32)
        for j in range(n_taps):
            lo = r0 + first_tap + j
            acc = acc + dww_ref[j:j + 1, cols] * zbuf_ref[s, lo:lo + n_rows, :]
        conv_ref[r0:r0 + n_rows, cols] = acc
        return acc

    return [functools.partial(unit, s, r0, min(CONV_ROWS, tm - r0))
            for s in range(n_slabs) for r0 in range(0, tm, CONV_ROWS)]


def _norm_swish(y, gain, bias):
    mu = jnp.mean(y, axis=-1, keepdims=True)
    cen = y - mu
    var = jnp.mean(cen * cen, axis=-1, keepdims=True)
    y = cen * lax.rsqrt(var + LN_EPS) * gain + bias
    return (y * jax.nn.sigmoid(y)).astype(BF16)


def _out_ffn_kernel(x_ref, ya_ref, zc_ref, z_hbm, dww_ref, dwb_ref, lng_ref, lnb_ref,
                    wo_hbm, g_ref, wup_hbm, cw_ref, cb_ref, wdn_hbm, fg_ref, o_ref,
                    wo_ref, wup_ref, wdn_ref, w_sems, u_ref,
                    zbuf_ref, conv_ref, ubuf_ref, carry_ref, gated_ref, znext_ref, znext_sem, *,
                    tiles_per_seq, n_tiles, n_chunks, final_norm):
    tm, d_model = x_ref.shape
    a = ya_ref.shape[1]
    width = zc_ref.shape[1]
    step = pl.program_id(0)
    fc = FFN_CHUNK
    n_slabs = 2 * fc // LANES
    conv_units = functools.partial(_depthwise_conv_units, dww_ref=dww_ref, dwb_ref=dwb_ref,
                                   zbuf_ref=zbuf_ref, conv_ref=conv_ref)

    def next_z_copy():
        nxt = jnp.minimum(step + 1, n_tiles - 1)
        row0 = pl.multiple_of((nxt % tiles_per_seq) * tm, tm)
        return pltpu.make_async_copy(z_hbm.at[nxt // tiles_per_seq, pl.ds(row0, tm), :],
                                     znext_ref, znext_sem)

    next_z_copy().start()
    d_ff = n_chunks * fc
    wup_copies = [(wup_hbm.at[:, pl.ds(part * d_ff + c * fc, fc)],
                   wup_ref.at[:, pl.ds((2 * c + part) * fc, fc)])
                  for c in range(n_chunks) for part in range(2)]
    _fetch_once([(wo_hbm, wo_ref), (wdn_hbm, wdn_ref)] + wup_copies, w_sems)

    @pl.when(step == 0)
    def _():
        for unit in conv_units(jnp.zeros((CONV_HALO, width), F32), zc_ref[...].astype(F32)):
            unit()

    @pl.when(step % tiles_per_seq == 0)
    def _():
        carry_ref[...] = jnp.zeros(carry_ref.shape, F32)

    y_conv = _norm_swish(conv_ref[...], lng_ref[...], lnb_ref[...])

    row_blocks = [slice(r0, r0 + MATMUL_ROWS) for r0 in range(0, tm, MATMUL_ROWS)]
    h = x_ref[...] + jnp.concatenate(
        [jnp.concatenate(
            [jnp.dot(ya_ref[rows, :], wo_ref[0:a, cols], preferred_element_type=F32)
             + jnp.dot(y_conv[rows, :], wo_ref[a:, cols], preferred_element_type=F32)
             for cols in _column_blocks(d_model)], axis=-1)
         for rows in row_blocks], axis=0)
    u_ref[...] = _rms_normalize(h, g_ref[...], NORM_EPS).astype(BF16)

    def up_proj(c):
        return jnp.concatenate(
            [jnp.dot(u_ref[rows, :], wup_ref[:, 2 * fc * c:2 * fc * (c + 1)],
                     preferred_element_type=F32) for rows in row_blocks], axis=0)

    def gate(c, up):
        buf = ubuf_ref.at[c % 2]
        parts = []
        for s in range(n_slabs):
            sl = slice(s * LANES, (s + 1) * LANES)
            buf[s, 0:FFN_HALO, :] = carry_ref[c, :, sl]
            buf[s, FFN_HALO:FFN_HALO + tm, :] = up[:, sl]
            carry_ref[c, :, sl] = up[tm - FFN_HALO:, sl]
            wcols = slice(2 * fc * c + s * LANES, 2 * fc * c + (s + 1) * LANES)
            conv = jnp.broadcast_to(cb_ref[:, wcols], (tm, LANES))
            for j in range(3):
                lo = FFN_HALO - 2 + j
                conv = conv + cw_ref[j:j + 1, wcols] * buf[s, lo:lo + tm, :]
            parts.append(conv)
        conv = jnp.concatenate(parts, axis=-1)
        ga = conv[:, :fc]
        gated_ref[:, fc * c:fc * (c + 1)] = (ga * jax.nn.sigmoid(ga) * conv[:, fc:]).astype(BF16)

    up = up_proj(0)
    for c in range(n_chunks):
        up_next = up_proj(c + 1) if c + 1 < n_chunks else None
        gate(c, up)
        up = up_next

    next_z_copy().wait()
    for cols in _column_blocks(d_model):
        for rows in row_blocks:
            o_ref[rows, cols] = jnp.dot(gated_ref[rows, :], wdn_ref[:, cols],
                                        preferred_element_type=F32)

    next_starts_seq = (step + 1) % tiles_per_seq == 0
    halo = jnp.where(next_starts_seq, 0.0, zc_ref[tm - CONV_HALO:, :].astype(F32))
    done = None
    for unit in conv_units(halo, znext_ref[...].astype(F32)):
        done = unit(after=done)

    acc = h + o_ref[...]
    if final_norm:
        acc = _rms_normalize(acc, fg_ref[...], NORM_EPS)
    o_ref[...] = acc


def _out_ffn(x, ya, z, dw_w, dw_b, ln_g, ln_b, w_out, norm_g, w_up, conv_w, conv_b, w_down,
             final_g, *, tm, final_norm):
    bsz, seq, d_model = x.shape
    a, cwid = ya.shape[2], z.shape[2]
    d_ff = w_down.shape[0]
    n_taps = dw_w.shape[0]
    assert seq % tm == 0 and d_ff % FFN_CHUNK == 0 and cwid % LANES == 0
    assert n_taps - 1 <= CONV_HALO <= tm and tm % MATMUL_ROWS == 0
    assert CONV_ROWS % SUBLANES == 0 and tm % SUBLANES == 0
    n_chunks = d_ff // FFN_CHUNK
    tiles_per_seq = seq // tm
    n_tiles = bsz * tiles_per_seq
    kernel = functools.partial(_out_ffn_kernel, tiles_per_seq=tiles_per_seq, n_tiles=n_tiles,
                               n_chunks=n_chunks, final_norm=final_norm)
    rows = lambda t: (t // tiles_per_seq, t % tiles_per_seq, 0)
    tile = lambda w: pl.BlockSpec((None, tm, w), rows)
    return pl.pallas_call(
        kernel,
        grid=(n_tiles,),
        in_specs=[
            tile(d_model), tile(a), tile(cwid), _IN_HBM,
            _resident((n_taps, cwid)), _resident((1, cwid)), _resident((1, cwid)),
            _resident((1, cwid)),
            _IN_HBM,
            _resident((1, d_model)),
            _IN_HBM,
            _resident((3, 2 * d_ff)),
            _resident((1, 2 * d_ff)),
            _IN_HBM,
            _resident((1, d_model)),
        ],
        out_specs=tile(d_model),
        out_shape=jax.ShapeDtypeStruct((bsz, seq, d_model), F32),
        scratch_shapes=[
            pltpu.VMEM((a + cwid, d_model), BF16),
            pltpu.VMEM((d_model, 2 * d_ff), BF16),
            pltpu.VMEM((d_ff, d_model), BF16),
            pltpu.SemaphoreType.DMA((2 + 2 * n_chunks,)),
            pltpu.VMEM((tm, d_model), BF16),
            pltpu.VMEM((cwid // LANES, CONV_HALO + tm, LANES), F32),
            pltpu.VMEM((tm, cwid), F32),
            pltpu.VMEM((2, 2 * FFN_CHUNK // LANES, FFN_HALO + tm, LANES), F32),
            pltpu.VMEM((n_chunks, FFN_HALO, 2 * FFN_CHUNK), F32),
            pltpu.VMEM((tm, d_ff), BF16),
            pltpu.VMEM((tm, cwid), BF16),
            pltpu.SemaphoreType.DMA(()),
        ],
        compiler_params=pltpu.CompilerParams(
            dimension_semantics=("arbitrary",),
            vmem_limit_bytes=VMEM_LIMIT_BYTES),
        name="out_ffn",
    )(x, ya, z, z, dw_w, dw_b.reshape(1, cwid), ln_g.reshape(1, cwid), ln_b.reshape(1, cwid),
      w_out, norm_g.reshape(1, d_model), w_up, conv_w, conv_b.reshape(1, 2 * d_ff),
      w_down, final_g.reshape(1, d_model))


def _group_ffn_columns(t, d_ff):
    lead = t.shape[:-1]
    n = d_ff // FFN_CHUNK
    t = t.reshape(lead + (2, n, FFN_CHUNK))
    return jnp.swapaxes(t, -3, -2).reshape(lead + (2 * d_ff,))


def _lambda_init(layer_idx):
    return 0.8 - 0.6 * math.exp(-0.3 * layer_idx)


def kernel(x, attn_norm_g, w_in, lambda_q1, lambda_k1, lambda_q2, lambda_k2, subln_g, dw_conv_w, dw_conv_b, conv_ln_g, conv_ln_b, w_out, ffn_norm_g, w_up, ffn_conv_w, ffn_conv_b, w_down, final_norm_g):
    depth = w_in.shape[0]
    d_model = x.shape[-1]
    conv_width = dw_conv_w.shape[-1]
    attn_width = d_model - conv_width
    d_ff = w_down.shape[1]
    h = x
    for l in range(depth):
        q, k, v, z = _in_proj(h, attn_norm_g[l], w_in[l].astype(BF16),
                              attn_width=attn_width, conv_width=conv_width, tm=1024)
        y_attn = _diff_attention(q, k, v, lambda_q1[l], lambda_k1[l], lambda_q2[l], lambda_k2[l],
                                 subln_g[l], _lambda_init(l), tile=256)
        h = _out_ffn(
            h, y_attn, z, dw_conv_w[l], dw_conv_b[l], conv_ln_g[l], conv_ln_b[l],
            w_out[l].astype(BF16), ffn_norm_g[l],
            w_up[l].astype(BF16),
            _group_ffn_columns(ffn_conv_w[l], d_ff), _group_ffn_columns(ffn_conv_b[l], d_ff),
            w_down[l].astype(BF16), final_norm_g, tm=512, final_norm=(l == depth - 1))
    return h
```

```python
import functools
import math

import jax
import jax.numpy as jnp
from jax import lax
from jax.experimental import pallas as pl
from jax.experimental.pallas import tpu as pltpu

F32 = jnp.float32
BF16 = jnp.bfloat16

HEAD_DIM = 64
HEAD_WIDTH = 2 * HEAD_DIM
NORM_EPS = 1e-6
LN_EPS = 1e-5
NEG_INF = -1e30
LOG2_E = math.log2(math.e)

SUM_ROWS = 16
LANES = 128
CONV_HALO = 32
SUBLANES = 8
CONV_ROWS = 24
FFN_HALO = 8
FFN_CHUNK = 256
MATMUL_COLS = 512
MATMUL_ROWS = 256
VMEM_LIMIT_BYTES = 56 * 1024 * 1024


def _rms_normalize(x, gain, eps):
    ms = jnp.mean(x * x, axis=-1, keepdims=True)
    return x * lax.rsqrt(ms + eps) * gain


def _column_blocks(width):
    return [slice(lo, lo + MATMUL_COLS) for lo in range(0, width, MATMUL_COLS)]


def _resident(shape):
    return pl.BlockSpec(shape, lambda *_: (0,) * len(shape), pipeline_mode=pl.Buffered(1))


_IN_HBM = pl.BlockSpec(memory_space=pl.ANY)


def _fetch_once(pairs, sems):
    @pl.when(pl.program_id(0) == 0)
    def _():
        copies = [pltpu.make_async_copy(src, dst, sems.at[i]) for i, (src, dst) in enumerate(pairs)]
        for cp in copies:
            cp.start()
        for cp in copies:
            cp.wait()


def _in_proj_kernel(x_ref, g_ref, w_hbm, q_ref, k_ref, v_ref, z_ref, w_ref, sems, u_ref, *,
                    attn_width, conv_width):
    _fetch_once([(w_hbm, w_ref)], sems)
    tm = x_ref.shape[0]
    a = attn_width
    for r0 in range(0, tm, MATMUL_ROWS):
        rows = slice(r0, r0 + MATMUL_ROWS)
        u_ref[rows, :] = _rms_normalize(x_ref[rows, :], g_ref[...], NORM_EPS).astype(BF16)

        def proj(lo, width):
            return jnp.dot(u_ref[rows, :], w_ref[:, lo:lo + width], preferred_element_type=F32)

        q_ref[rows, :] = (proj(0, a) * (HEAD_DIM ** -0.5 * LOG2_E)).astype(BF16)
        k_ref[rows, :] = proj(a, a).astype(BF16)
        v_ref[rows, :] = proj(2 * a, a).astype(BF16)
        val = proj(3 * a, conv_width)
        gate = proj(3 * a + conv_width, conv_width)
        z_ref[rows, :] = (val * jax.nn.sigmoid(gate)).astype(BF16)


def _in_proj(x, norm_g, w_in, *, attn_width, conv_width, tm):
    bsz, seq, d_model = x.shape
    in_width = w_in.shape[1]
    assert seq % tm == 0 and tm % MATMUL_ROWS == 0
    kernel = functools.partial(_in_proj_kernel, attn_width=attn_width, conv_width=conv_width)
    tiles_per_seq = seq // tm
    tile = lambda w: pl.BlockSpec((None, tm, w),
                                  lambda t: (t // tiles_per_seq, t % tiles_per_seq, 0))
    out = lambda w: jax.ShapeDtypeStruct((bsz, seq, w), BF16)
    return pl.pallas_call(
        kernel,
        grid=(bsz * tiles_per_seq,),
        in_specs=[tile(d_model), _resident((1, d_model)), _IN_HBM],
        out_specs=[tile(attn_width), tile(attn_width), tile(attn_width), tile(conv_width)],
        out_shape=[out(attn_width), out(attn_width), out(attn_width), out(conv_width)],
        scratch_shapes=[pltpu.VMEM((d_model, in_width), BF16), pltpu.SemaphoreType.DMA((1,)),
                        pltpu.VMEM((tm, d_model), BF16)],
        compiler_params=pltpu.CompilerParams(
            dimension_semantics=("arbitrary",),
            vmem_limit_bytes=VMEM_LIMIT_BYTES),
        name="in_proj",
    )(x, norm_g.reshape(1, d_model), w_in)


def _attn_kernel(lq1_ref, lk1_ref, lq2_ref, lk2_ref, q_ref, k_ref, v_ref, g_ref, o_ref,
                 vt_ref, *, tile, lambda_init):
    seq = k_ref.shape[0]
    n = seq // tile

    for j in range(n):
        vt = v_ref[j * tile:(j + 1) * tile, :].astype(F32).T.astype(BF16)
        vt_ref[j] = jnp.concatenate([vt, jnp.ones((SUM_ROWS, tile), BF16)], axis=0)

    lam = (jnp.exp(jnp.sum(lq1_ref[...] * lk1_ref[...], axis=-1, keepdims=True))
           - jnp.exp(jnp.sum(lq2_ref[...] * lk2_ref[...], axis=-1, keepdims=True))
           + lambda_init)
    lane = lax.broadcasted_iota(jnp.int32, (seq, HEAD_WIDTH), 1)
    kpos = lax.broadcasted_iota(jnp.int32, (tile, tile), 0)
    qpos = lax.broadcasted_iota(jnp.int32, (tile, tile), 1)
    causal = kpos <= qpos

    qf = q_ref[...].astype(F32)
    q_comp = (jnp.where(lane < HEAD_DIM, qf, 0.0).astype(BF16),
              jnp.where(lane >= HEAD_DIM, qf, 0.0).astype(BF16))
    m = [[jnp.full((1, tile), NEG_INF, F32) for _ in range(n)] for _ in range(2)]
    acc = [[jnp.zeros((HEAD_WIDTH + SUM_ROWS, tile), F32) for _ in range(n)] for _ in range(2)]

    for j in range(n):
        kb = k_ref[j * tile:(j + 1) * tile, :]
        nb = n - j
        q_cat = jnp.concatenate([q_comp[0][j * tile:, :], q_comp[1][j * tile:, :]], axis=0)
        s_all = lax.dot_general(kb, q_cat, (((1,), (1,)), ((), ())),
                                preferred_element_type=F32)
        probs, alphas = [], []
        for c in range(2):
            for b in range(j, n):
                col = (c * nb + b - j) * tile
                s = s_all[:, col:col + tile]
                if b == j:
                    s = jnp.where(causal, s, NEG_INF)
                m_new = jnp.maximum(m[c][b], jnp.max(s, axis=0, keepdims=True))
                alphas.append(jnp.exp2(m[c][b] - m_new))
                probs.append(jnp.exp2(s - m_new).astype(BF16))
                m[c][b] = m_new
        pv_all = jnp.dot(vt_ref[j], jnp.concatenate(probs, axis=1), preferred_element_type=F32)
        for c in range(2):
            for b in range(j, n):
                idx = c * nb + b - j
                acc[c][b] = alphas[idx] * acc[c][b] + pv_all[:, idx * tile:(idx + 1) * tile]

        num = [acc[c][j][:HEAD_WIDTH] for c in range(2)]
        den = [acc[c][j][HEAD_WIDTH:HEAD_WIDTH + 1] for c in range(2)]
        o = num[0] / den[0] - lam * (num[1] / den[1])
        o = o * lax.rsqrt(jnp.mean(o * o, axis=0, keepdims=True) + NORM_EPS)
        o_ref[j * tile:(j + 1) * tile, :] = (o.T * g_ref[...] * (1.0 - lambda_init)).astype(BF16)


def _diff_attention(q, k, v, lq1, lk1, lq2, lk2, subln_g, lambda_init, *, tile):
    bsz, seq, width = q.shape
    n_heads = width // HEAD_WIDTH
    assert seq % tile == 0
    kernel = functools.partial(_attn_kernel, tile=tile, lambda_init=lambda_init)
    lam_spec = _resident((1, HEAD_DIM))
    head = pl.BlockSpec((None, seq, HEAD_WIDTH), lambda b, h: (b, 0, h))
    return pl.pallas_call(
        kernel,
        grid=(bsz, n_heads),
        in_specs=[lam_spec, lam_spec, lam_spec, lam_spec, head, head, head,
                  _resident((1, HEAD_WIDTH))],
        out_specs=head,
        out_shape=jax.ShapeDtypeStruct((bsz, seq, width), BF16),
        scratch_shapes=[pltpu.VMEM((seq // tile, HEAD_WIDTH + SUM_ROWS, tile), BF16)],
        compiler_params=pltpu.CompilerParams(
            dimension_semantics=("arbitrary", "arbitrary"),
            vmem_limit_bytes=VMEM_LIMIT_BYTES),
        name="diff_attention",
    )(lq1.reshape(1, HEAD_DIM), lk1.reshape(1, HEAD_DIM), lq2.reshape(1, HEAD_DIM),
      lk2.reshape(1, HEAD_DIM), q, k, v, subln_g.reshape(1, HEAD_WIDTH))


def _depthwise_conv_units(halo, z, dww_ref, dwb_ref, zbuf_ref, conv_ref):
    tm, width = z.shape
    n_taps = dww_ref.shape[0]
    n_slabs = width // LANES
    for s in range(n_slabs):
        cols = slice(s * LANES, (s + 1) * LANES)
        zbuf_ref[s, 0:CONV_HALO, :] = halo[:, cols]
        zbuf_ref[s, CONV_HALO:CONV_HALO + tm, :] = z[:, cols]
    first_tap = CONV_HALO - (n_taps - 1)

    def unit(s, r0, n_rows, after=None):
        cols = slice(s * LANES, (s + 1) * LANES)
        acc = jnp.broadcast_to(dwb_ref[:, cols], (n_rows, LANES))
        if after is not None:
            zero = (lax.bitcast_convert_type(after[0:SUBLANES], jnp.uint32) >> 16) >> 16
            zero = jnp.concatenate([zero] * (n_rows // SUBLANES), axis=0)
            acc = lax.bitcast_convert_type(lax.bitcast_convert_type(acc, jnp.uint32) + zero, F32)
        for j in range(n_taps):
            lo = r0 + first_tap + j
            acc = acc + dww_ref[j:j + 1, cols] * zbuf_ref[s, lo:lo + n_rows, :]
        conv_ref[r0:r0 + n_rows, cols] = acc
        return acc

    return [functools.partial(unit, s, r0, min(CONV_ROWS, tm - r0))
            for s in range(n_slabs) for r0 in range(0, tm, CONV_ROWS)]


def _norm_swish(y, gain, bias):
    mu = jnp.mean(y, axis=-1, keepdims=True)
    cen = y - mu
    var = jnp.mean(cen * cen, axis=-1, keepdims=True)
    y = cen * lax.rsqrt(var + LN_EPS) * gain + bias
    return (y * jax.nn.sigmoid(y)).astype(BF16)


def _out_ffn_kernel(x_ref, ya_ref, zc_ref, z_hbm, dww_ref, dwb_ref, lng_ref, lnb_ref,
                    wo_hbm, g_ref, wup_hbm, cw_ref, cb_ref, wdn_hbm, fg_ref, o_ref,
                    wo_ref, wup_ref, wdn_ref, w_sems, u_ref,
                    zbuf_ref, conv_ref, ubuf_ref, carry_ref, gated_ref, znext_ref, znext_sem, *,
                    tiles_per_seq, n_tiles, n_chunks, final_norm):
    tm, d_model = x_ref.shape
    a = ya_ref.shape[1]
    width = zc_ref.shape[1]
    step = pl.program_id(0)
    fc = FFN_CHUNK
    n_slabs = 2 * fc // LANES
    conv_units = functools.partial(_depthwise_conv_units, dww_ref=dww_ref, dwb_ref=dwb_ref,
                                   zbuf_ref=zbuf_ref, conv_ref=conv_ref)

    def next_z_copy():
        nxt = jnp.minimum(step + 1, n_tiles - 1)
        row0 = pl.multiple_of((nxt % tiles_per_seq) * tm, tm)
        return pltpu.make_async_copy(z_hbm.at[nxt // tiles_per_seq, pl.ds(row0, tm), :],
                                     znext_ref, znext_sem)

    next_z_copy().start()
    d_ff = n_chunks * fc
    wup_copies = [(wup_hbm.at[:, pl.ds(part * d_ff + c * fc, fc)],
                   wup_ref.at[:, pl.ds((2 * c + part) * fc, fc)])
                  for c in range(n_chunks) for part in range(2)]
    _fetch_once([(wo_hbm, wo_ref), (wdn_hbm, wdn_ref)] + wup_copies, w_sems)

    @pl.when(step == 0)
    def _():
        for unit in conv_units(jnp.zeros((CONV_HALO, width), F32), zc_ref[...].astype(F32)):
            unit()

    @pl.when(step % tiles_per_seq == 0)
    def _():
        carry_ref[...] = jnp.zeros(carry_ref.shape, F32)

    y_conv = _norm_swish(conv_ref[...], lng_ref[...], lnb_ref[...])

    row_blocks = [slice(r0, r0 + MATMUL_ROWS) for r0 in range(0, tm, MATMUL_ROWS)]
    h = x_ref[...] + jnp.concatenate(
        [jnp.concatenate(
            [jnp.dot(ya_ref[rows, :], wo_ref[0:a, cols], preferred_element_type=F32)
             + jnp.dot(y_conv[rows, :], wo_ref[a:, cols], preferred_element_type=F32)
             for cols in _column_blocks(d_model)], axis=-1)
         for rows in row_blocks], axis=0)
    u_ref[...] = _rms_normalize(h, g_ref[...], NORM_EPS).astype(BF16)

    def up_proj(c):
        return jnp.concatenate(
            [jnp.dot(u_ref[rows, :], wup_ref[:, 2 * fc * c:2 * fc * (c + 1)],
                     preferred_element_type=F32) for rows in row_blocks], axis=0)

    def gate(c, up):
        buf = ubuf_ref.at[c % 2]
        parts = []
        for s in range(n_slabs):
            sl = slice(s * LANES, (s + 1) * LANES)
            buf[s, 0:FFN_HALO, :] = carry_ref[c, :, sl]
            buf[s, FFN_HALO:FFN_HALO + tm, :] = up[:, sl]
            carry_ref[c, :, sl] = up[tm - FFN_HALO:, sl]
            wcols = slice(2 * fc * c + s * LANES, 2 * fc * c + (s + 1) * LANES)
            conv = jnp.broadcast_to(cb_ref[:, wcols], (tm, LANES))
            for j in range(3):
                lo = FFN_HALO - 2 + j
                conv = conv + cw_ref[j:j + 1, wcols] * buf[s, lo:lo + tm, :]
            parts.append(conv)
        conv = jnp.concatenate(parts, axis=-1)
        ga = conv[:, :fc]
        gated_ref[:, fc * c:fc * (c + 1)] = (ga * jax.nn.sigmoid(ga) * conv[:, fc:]).astype(BF16)

    up = up_proj(0)
    for c in range(n_chunks):
        up_next = up_proj(c + 1) if c + 1 < n_chunks else None
        gate(c, up)
        up = up_next

    next_z_copy().wait()
    for rows in row_blocks:
        for cols in _column_blocks(d_model):
            o_ref[rows, cols] = jnp.dot(gated_ref[rows, :], wdn_ref[:, cols],
                                        preferred_element_type=F32)

    next_starts_seq = (step + 1) % tiles_per_seq == 0
    halo = jnp.where(next_starts_seq, 0.0, zc_ref[tm - CONV_HALO:, :].astype(F32))
    done = None
    for unit in conv_units(halo, znext_ref[...].astype(F32)):
        done = unit(after=done)

    acc = h + o_ref[...]
    if final_norm:
        acc = _rms_normalize(acc, fg_ref[...], NORM_EPS)
    o_ref[...] = acc


def _out_ffn(x, ya, z, dw_w, dw_b, ln_g, ln_b, w_out, norm_g, w_up, conv_w, conv_b, w_down,
             final_g, *, tm, final_norm):
    bsz, seq, d_model = x.shape
    a, cwid = ya.shape[2], z.shape[2]
    d_ff = w_down.shape[0]
    n_taps = dw_w.shape[0]
    assert seq % tm == 0 and d_ff % FFN_CHUNK == 0 and cwid % LANES == 0
    assert n_taps - 1 <= CONV_HALO <= tm and tm % MATMUL_ROWS == 0
    assert CONV_ROWS % SUBLANES == 0 and tm % SUBLANES == 0
    n_chunks = d_ff // FFN_CHUNK
    tiles_per_seq = seq // tm
    n_tiles = bsz * tiles_per_seq
    kernel = functools.partial(_out_ffn_kernel, tiles_per_seq=tiles_per_seq, n_tiles=n_tiles,
                               n_chunks=n_chunks, final_norm=final_norm)
    rows = lambda t: (t // tiles_per_seq, t % tiles_per_seq, 0)
    tile = lambda w: pl.BlockSpec((None, tm, w), rows)
    return pl.pallas_call(
        kernel,
        grid=(n_tiles,),
        in_specs=[
            tile(d_model), tile(a), tile(cwid), _IN_HBM,
            _resident((n_taps, cwid)), _resident((1, cwid)), _resident((1, cwid)),
            _resident((1, cwid)),
            _IN_HBM,
            _resident((1, d_model)),
            _IN_HBM,
            _resident((3, 2 * d_ff)),
            _resident((1, 2 * d_ff)),
            _IN_HBM,
            _resident((1, d_model)),
        ],
        out_specs=tile(d_model),
        out_shape=jax.ShapeDtypeStruct((bsz, seq, d_model), F32),
        scratch_shapes=[
            pltpu.VMEM((a + cwid, d_model), BF16),
            pltpu.VMEM((d_model, 2 * d_ff), BF16),
            pltpu.VMEM((d_ff, d_model), BF16),
            pltpu.SemaphoreType.DMA((2 + 2 * n_chunks,)),
            pltpu.VMEM((tm, d_model), BF16),
            pltpu.VMEM((cwid // LANES, CONV_HALO + tm, LANES), F32),
            pltpu.VMEM((tm, cwid), F32),
            pltpu.VMEM((2, 2 * FFN_CHUNK // LANES, FFN_HALO + tm, LANES), F32),
            pltpu.VMEM((n_chunks, FFN_HALO, 2 * FFN_CHUNK), F32),
            pltpu.VMEM((tm, d_ff), BF16),
            pltpu.VMEM((tm, cwid), BF16),
            pltpu.SemaphoreType.DMA(()),
        ],
        compiler_params=pltpu.CompilerParams(
            dimension_semantics=("arbitrary",),
            vmem_limit_bytes=VMEM_LIMIT_BYTES),
        name="out_ffn",
    )(x, ya, z, z, dw_w, dw_b.reshape(1, cwid), ln_g.reshape(1, cwid), ln_b.reshape(1, cwid),
      w_out, norm_g.reshape(1, d_model), w_up, conv_w, conv_b.reshape(1, 2 * d_ff),
      w_down, final_g.reshape(1, d_model))


def _group_ffn_columns(t, d_ff):
    lead = t.shape[:-1]
    n = d_ff // FFN_CHUNK
    t = t.reshape(lead + (2, n, FFN_CHUNK))
    return jnp.swapaxes(t, -3, -2).reshape(lead + (2 * d_ff,))


def _lambda_init(layer_idx):
    return 0.8 - 0.6 * math.exp(-0.3 * layer_idx)


def kernel(x, attn_norm_g, w_in, lambda_q1, lambda_k1, lambda_q2, lambda_k2, subln_g, dw_conv_w, dw_conv_b, conv_ln_g, conv_ln_b, w_out, ffn_norm_g, w_up, ffn_conv_w, ffn_conv_b, w_down, final_norm_g):
    depth = w_in.shape[0]
    d_model = x.shape[-1]
    conv_width = dw_conv_w.shape[-1]
    attn_width = d_model - conv_width
    d_ff = w_down.shape[1]
    h = x
    for l in range(depth):
        q, k, v, z = _in_proj(h, attn_norm_g[l], w_in[l].astype(BF16),
                              attn_width=attn_width, conv_width=conv_width, tm=1024)
        y_attn = _diff_attention(q, k, v, lambda_q1[l], lambda_k1[l], lambda_q2[l], lambda_k2[l],
                                 subln_g[l], _lambda_init(l), tile=256)
        h = _out_ffn(
            h, y_attn, z, dw_conv_w[l], dw_conv_b[l], conv_ln_g[l], conv_ln_b[l],
            w_out[l].astype(BF16), ffn_norm_g[l],
            w_up[l].astype(BF16),
            _group_ffn_columns(ffn_conv_w[l], d_ff), _group_ffn_columns(ffn_conv_b[l], d_ff),
            w_down[l].astype(BF16), final_norm_g, tm=512, final_norm=(l == depth - 1))
    return h
```
